```python
import math
import jax, jax.numpy as jnp
from jax import lax
import numpy as np


D_MODEL = 2048
BATCH = 8
SEQ = 2048
DEPTH = 1

MEM_TOKENS = 256
RET_HEADS = 8
RET_QK_DIM = 128
RET_V_DIM = 128
RET_CHUNK = 128
DIL_PATTERNS = ((128, 1), (512, 4), (2048, 16))
DIL_HEADS_PER_GROUP = 4
DIL_HEAD_DIM = 128
DIL_BLOCK = 128
ROPE_THETA = 10000.0
MEM_HEADS = 4
MEM_HEAD_DIM = D_MODEL // MEM_HEADS
PEER_HEADS = 8
PEER_N_KEYS = 128
PEER_N_EXPERTS = PEER_N_KEYS * PEER_N_KEYS
PEER_QUERY_DIM = 256
PEER_TOPK = 16
PEER_TOKEN_BLOCK = 128
NORM_EPS = 1e-6

RET_QK_W = RET_HEADS * RET_QK_DIM
RET_V_W = RET_HEADS * RET_V_DIM
DIL_N_GROUPS = len(DIL_PATTERNS)
DIL_QKV_W = DIL_N_GROUPS * DIL_HEADS_PER_GROUP * DIL_HEAD_DIM
DIL_OUT_W = DIL_HEADS_PER_GROUP * DIL_HEAD_DIM
IN_SPLIT_SIZES = (RET_QK_W, RET_QK_W, RET_V_W, RET_V_W, DIL_QKV_W, DIL_QKV_W, DIL_QKV_W, D_MODEL, D_MODEL)
IN_WIDTH = sum(IN_SPLIT_SIZES)
IN_SPLIT_POINTS = tuple(int(c) for c in np.cumsum(IN_SPLIT_SIZES)[:-1])

kernel_name = 'hybrid_retention_dilated_peer_block'


def rms_norm(x, g):
    xf = x.astype(jnp.float32)
    y = xf * lax.rsqrt(jnp.mean(xf * xf, axis=-1, keepdims=True) + NORM_EPS)
    return (y * g.astype(jnp.float32)).astype(x.dtype)


def rotary(x, pos):
    dh = x.shape[-1]
    half = dh // 2
    inv = 1.0 / (ROPE_THETA ** (jnp.arange(half, dtype=jnp.float32) / half))
    ang = pos.astype(jnp.float32)[..., None] * inv
    cos = jnp.cos(ang)[:, :, None, :]
    sin = jnp.sin(ang)[:, :, None, :]
    xf = x.astype(jnp.float32)
    x1, x2 = xf[..., :half], xf[..., half:]
    return jnp.concatenate([x1 * cos - x2 * sin, x2 * cos + x1 * sin], axis=-1).astype(x.dtype)


def retnet_rotate(x, pos):
    dh = x.shape[-1]
    angle = 1.0 / (10000.0 ** jnp.linspace(0.0, 1.0, dh // 2, dtype=jnp.float32))
    angle = jnp.repeat(angle, 2)
    ang = pos.astype(jnp.float32)[..., None] * angle
    cos = jnp.cos(ang)[:, :, None, :]
    sin = jnp.sin(ang)[:, :, None, :]
    xf = x.astype(jnp.float32)
    rot = jnp.stack([-xf[..., 1::2], xf[..., 0::2]], axis=-1).reshape(xf.shape)
    return xf * cos + rot * sin


def retention(q, k, v, pos):
    b, s, h, dk = q.shape
    dv = v.shape[-1]
    c = RET_CHUNK
    n = s // c
    q = retnet_rotate(q, pos)
    k = retnet_rotate(k, pos) * (dk ** -0.5)
    v = v.astype(jnp.float32)
    to_chunks = lambda t: t.reshape(b, n, c, h, t.shape[-1]).transpose(0, 3, 1, 2, 4)
    qc, kc, vc = to_chunks(q), to_chunks(k), to_chunks(v)
    log_g = jnp.log1p(-jnp.exp2(-5.0 - jnp.arange(h, dtype=jnp.float32)))
    i = jnp.arange(c, dtype=jnp.float32)
    diff = i[:, None] - i[None, :]
    dmat = jnp.where(diff >= 0, jnp.exp(log_g[:, None, None] * jnp.maximum(diff, 0.0)), 0.0)
    scores = jnp.einsum('bhncd,bhnjd->bhncj', qc, kc) * dmat[None, :, None]
    inner = jnp.einsum('bhncj,bhnje->bhnce', scores, vc)
    k_w = jnp.exp(log_g[:, None] * (c - 1 - i))
    kv = jnp.einsum('bhncd,bhnce->bhnde', kc * k_w[None, :, None, :, None], vc)
    chunk_decay = jnp.exp(log_g * c)[None, :, None, None]

    def step(state, kv_n):
        return chunk_decay * state + kv_n, state

    init = jnp.zeros((b, h, dk, dv), kv.dtype)
    _, prev = lax.scan(step, init, jnp.moveaxis(kv, 2, 0))
    prev = jnp.moveaxis(prev, 0, 2)
    q_w = jnp.exp(log_g[:, None] * (i + 1.0))
    cross = jnp.einsum('bhncd,bhnde->bhnce', qc * q_w[None, :, None, :, None], prev)
    out = (inner + cross).transpose(0, 2, 3, 1, 4).reshape(b, s, h, dv)
    out = out * lax.rsqrt(jnp.mean(out * out, axis=-1, keepdims=True) + NORM_EPS)
    return out


def banded_attention(q, k, v, window):
    *lead, L, dh = q.shape
    blk = DIL_BLOCK
    nb = -(-L // blk)
    pad = nb * blk - L
    padw = [(0, 0)] * len(lead) + [(0, pad), (0, 0)]
    qb = jnp.pad(q, padw).reshape(*lead, nb, blk, dh)
    kb = jnp.pad(k, padw).reshape(*lead, nb, blk, dh)
    vb = jnp.pad(v, padw).reshape(*lead, nb, blk, dh)
    shift = lambda t: jnp.concatenate([jnp.zeros_like(t[..., :1, :, :]), t[..., :-1, :, :]], axis=-3)
    kcat = jnp.concatenate([shift(kb), kb], axis=-2)
    vcat = jnp.concatenate([shift(vb), vb], axis=-2)
    s = jnp.einsum('...nqd,...nkd->...nqk', qb, kcat).astype(jnp.float32) * (dh ** -0.5)
    nidx = jnp.arange(nb)[:, None, None]
    qpos = nidx * blk + jnp.arange(blk)[None, :, None]
    kpos = (nidx - 1) * blk + jnp.arange(2 * blk)[None, None, :]
    dist = qpos - kpos
    mask = (dist >= 0) & (dist <= window) & (kpos >= 0)
    s = jnp.where(mask, s, -jnp.inf)
    lse = jax.nn.logsumexp(s, axis=-1, keepdims=True)
    p = jnp.exp(s - lse)
    out = jnp.einsum('...nqk,...nkd->...nqd', p.astype(vcat.dtype), vcat)
    out = out.reshape(*lead, nb * blk, dh)[..., :L, :]
    lse = lse[..., 0].reshape(*lead, nb * blk)[..., :L]
    return out, lse


def dilated_attention(q, k, v):
    b, s, _, dh = q.shape
    hg = DIL_HEADS_PER_GROUP
    outs, lses = [], []
    for g, (w, r) in enumerate(DIL_PATTERNS):
        sl = slice(g * hg, (g + 1) * hg)
        sub = lambda t: t[:, :, sl].reshape(b, s // r, r, hg, dh).transpose(0, 2, 3, 1, 4)
        o, l = banded_attention(sub(q), sub(k), sub(v), w // r)
        outs.append(o.transpose(0, 3, 1, 2, 4).reshape(b, s, hg, dh))
        lses.append(l.transpose(0, 3, 1, 2).reshape(b, s, hg))
    outs = jnp.stack(outs, axis=0)
    wts = jax.nn.softmax(jnp.stack(lses, axis=0), axis=0)
    return jnp.sum(wts[..., None] * outs.astype(jnp.float32), axis=0)


def memory_cross_attention(h, memn, w_q, w_kv, w_o):
    b, s, d = h.shape
    m = memn.shape[1]
    q = (h @ w_q).reshape(b, s, MEM_HEADS, MEM_HEAD_DIM)
    k, v = jnp.split(memn @ w_kv, 2, axis=-1)
    k = k.reshape(b, m, MEM_HEADS, MEM_HEAD_DIM)
    v = v.reshape(b, m, MEM_HEADS, MEM_HEAD_DIM)
    sc = jnp.einsum('bshd,bmhd->bhsm', q, k).astype(jnp.float32) * (MEM_HEAD_DIM ** -0.5)
    p = jax.nn.softmax(sc, axis=-1)
    o = jnp.einsum('bhsm,bmhd->bshd', p.astype(v.dtype), v).reshape(b, s, d)
    return o @ w_o


def peer_ffn(h, w_q, subkeys, u_tab, v_tab):
    b, s, d = h.shape
    t = b * s
    ht = h.reshape(t, d)
    qh = (ht @ w_q).reshape(t, PEER_HEADS, 2, PEER_QUERY_DIM // 2)
    sc = jnp.einsum('thpd,hpnd->thpn', qh, subkeys).astype(jnp.float32)
    v_top, i_top = lax.top_k(sc, PEER_TOPK)
    combo = (v_top[:, :, 0, :, None] + v_top[:, :, 1, None, :]).reshape(t, PEER_HEADS, PEER_TOPK * PEER_TOPK)
    c_val, c_idx = lax.top_k(combo, PEER_TOPK)
    i1 = jnp.take_along_axis(i_top[:, :, 0, :], c_idx // PEER_TOPK, axis=-1)
    i2 = jnp.take_along_axis(i_top[:, :, 1, :], c_idx % PEER_TOPK, axis=-1)
    e_idx = i1 * PEER_N_KEYS + i2
    gates = jax.nn.softmax(c_val, axis=-1)
    nblk = t // PEER_TOKEN_BLOCK
    kk = PEER_HEADS * PEER_TOPK
    xb = ht.reshape(nblk, PEER_TOKEN_BLOCK, d)
    ib = e_idx.reshape(nblk, PEER_TOKEN_BLOCK, kk)
    gb = gates.reshape(nblk, PEER_TOKEN_BLOCK, kk).astype(h.dtype)

    def expert_block(args):
        x_, i_, g_ = args
        u = u_tab[i_]
        a = jax.nn.gelu(jnp.einsum('td,tkd->tk', x_, u), approximate=False)
        return jnp.einsum('tk,tkd->td', a * g_, v_tab[i_])

    return lax.map(expert_block, (xb, ib, gb)).reshape(b, s, d)


def setup_inputs(seed: int = 0) -> dict:
    key = jax.random.key(seed)
    ks = jax.random.split(key, 20)
    f32 = jnp.float32
    nrm = lambda k, shape, scale: jax.random.normal(k, shape, f32) * scale
    gain = lambda k, shape: 1.0 + 0.02 * jax.random.normal(k, shape, f32)
    x = nrm(ks[0], (BATCH, SEQ, D_MODEL), 1.0)
    mem = nrm(ks[1], (BATCH, MEM_TOKENS, D_MODEL), 1.0)
    offset = jax.random.randint(ks[2], (BATCH, 1), 0, 4096, dtype=jnp.int32)
    positions = offset + jnp.arange(SEQ, dtype=jnp.int32)[None, :]
    return {
        'x': x,
        'mem': mem,
        'positions': positions,
        'g_mix': gain(ks[3], (DEPTH, D_MODEL)),
        'w_in': nrm(ks[4], (DEPTH, D_MODEL, IN_WIDTH), D_MODEL ** -0.5),
        'w_br_ret': nrm(ks[5], (DEPTH, RET_V_W, D_MODEL), RET_V_W ** -0.5),
        'w_br_dil': nrm(ks[6], (DEPTH, DIL_OUT_W, D_MODEL), DIL_OUT_W ** -0.5),
        'w_out': nrm(ks[7], (DEPTH, D_MODEL, D_MODEL), D_MODEL ** -0.5),
        'g_cross': gain(ks[8], (DEPTH, D_MODEL)),
        'g_mem': gain(ks[9], (DEPTH, D_MODEL)),
        'w_q_mem': nrm(ks[10], (DEPTH, D_MODEL, D_MODEL), D_MODEL ** -0.5),
        'w_kv_mem': nrm(ks[11], (DEPTH, D_MODEL, 2 * D_MODEL), D_MODEL ** -0.5),
        'w_o_mem': nrm(ks[12], (DEPTH, D_MODEL, D_MODEL), D_MODEL ** -0.5),
        'g_ffn': gain(ks[13], (DEPTH, D_MODEL)),
        'w_peer_q': nrm(ks[14], (DEPTH, D_MODEL, PEER_HEADS * PEER_QUERY_DIM), D_MODEL ** -0.5),
        'peer_subkeys': nrm(ks[15], (DEPTH, PEER_HEADS, 2, PEER_N_KEYS, PEER_QUERY_DIM // 2), (PEER_QUERY_DIM // 2) ** -0.5),
        'peer_u': nrm(ks[16], (DEPTH, PEER_N_EXPERTS, D_MODEL), D_MODEL ** -0.5),
        'peer_v': nrm(ks[17], (DEPTH, PEER_N_EXPERTS, D_MODEL), PEER_HEADS ** -0.5),
        'g_final': gain(ks[18], (D_MODEL,)),
    }


def reference(x, mem, positions, g_mix, w_in, w_br_ret, w_br_dil, w_out, g_cross, g_mem, w_q_mem, w_kv_mem, w_o_mem, g_ffn, w_peer_q, peer_subkeys, peer_u, peer_v, g_final):
    b, s, d = x.shape
    for l in range(DEPTH):
        h = rms_norm(x, g_mix[l])
        proj = h @ w_in[l]
        rq, rk, rv, rg, dq, dk, dv, gate_ret, gate_dil = jnp.split(proj, IN_SPLIT_POINTS, axis=-1)
        ret = retention(rq.reshape(b, s, RET_HEADS, RET_QK_DIM), rk.reshape(b, s, RET_HEADS, RET_QK_DIM),
                        rv.reshape(b, s, RET_HEADS, RET_V_DIM), positions)
        ret = (ret.reshape(b, s, RET_V_W) * jax.nn.silu(rg.astype(jnp.float32))).astype(x.dtype)
        nh = DIL_N_GROUPS * DIL_HEADS_PER_GROUP
        dq = rotary(dq.reshape(b, s, nh, DIL_HEAD_DIM), positions)
        dk = rotary(dk.reshape(b, s, nh, DIL_HEAD_DIM), positions)
        dv = dv.reshape(b, s, nh, DIL_HEAD_DIM)
        dil = dilated_attention(dq, dk, dv).reshape(b, s, DIL_OUT_W).astype(x.dtype)
        merged = jax.nn.sigmoid(gate_ret) * (ret @ w_br_ret[l]) + jax.nn.sigmoid(gate_dil) * (dil @ w_br_dil[l])
        x = x + merged @ w_out[l]
        hc = rms_norm(x, g_cross[l])
        memn = rms_norm(mem, g_mem[l])
        x = x + memory_cross_attention(hc, memn, w_q_mem[l], w_kv_mem[l], w_o_mem[l])
        hf = rms_norm(x, g_ffn[l])
        x = x + peer_ffn(hf, w_peer_q[l], peer_subkeys[l], peer_u[l], peer_v[l])
    return rms_norm(x, g_final)
```

```python
import functools
import math

import jax
import jax.numpy as jnp
import numpy as np
from jax import lax
from jax.experimental import pallas as pl
from jax.experimental.pallas import tpu as pltpu

F32 = jnp.float32
BF16 = jnp.bfloat16

NORM_EPS = 1e-6
ROPE_THETA = 10000.0
HEAD = 128
RET_HEADS = 8
RET_CHUNK = 128
DIL_PATTERNS = ((128, 1), (512, 4), (2048, 16))
DIL_HEADS_PER_GROUP = 4
DIL_BLOCK = 128
MEM_HEADS = 4
PEER_HEADS = 8
PEER_N_KEYS = 128
PEER_TOPK = 16

V7X_VMEM_BYTES = 64 * 1024 * 1024
VMEM_LIMIT = 56 * 1024 * 1024

NT_DIMS = (((1,), (1,)), ((), ()))
TN_DIMS = (((0,), (0,)), ((), ()))


def _params(*sem):
    return pltpu.CompilerParams(dimension_semantics=sem, vmem_limit_bytes=VMEM_LIMIT)


def _norm_rows(x_ref, g_ref, hn_ref, rows):
    tm = x_ref.shape[0]

    def body(c, _):
        sl = pl.ds(pl.multiple_of(c * rows, rows), rows)
        x = x_ref[sl, :].astype(F32)
        ms = jnp.mean(x * x, axis=-1, keepdims=True)
        hn_ref[sl, :] = ((x * lax.rsqrt(ms + NORM_EPS)) * g_ref[...]).astype(BF16)
        return 0

    lax.fori_loop(0, tm // rows, body, 0)


def _norm_matmul_kernel(x_ref, g_ref, w_ref, o_ref, hn_ref):
    @pl.when(pl.program_id(1) == 0)
    def _():
        _norm_rows(x_ref, g_ref, hn_ref, 256)

    o_ref[...] = jnp.dot(hn_ref[...], w_ref[...], preferred_element_type=F32).astype(o_ref.dtype)


def _norm_matmul(x, g, w, *, tm, tn, out_dtype, emit_hn=False):
    m, k = x.shape
    n = w.shape[1]
    assert m % tm == 0 and n % tn == 0
    in_specs = [
        pl.BlockSpec((tm, k), lambda i, j: (i, 0)),
        pl.BlockSpec((1, k), lambda i, j: (0, 0)),
        pl.BlockSpec((k, tn), lambda i, j: (0, j)),
    ]
    o_spec = pl.BlockSpec((tm, tn), lambda i, j: (i, j))
    o_shape = jax.ShapeDtypeStruct((m, n), out_dtype)
    if emit_hn:
        out_specs = [o_spec, pl.BlockSpec((tm, k), lambda i, j: (i, 0))]
        out_shape = [o_shape, jax.ShapeDtypeStruct((m, k), BF16)]
        scratch = []
    else:
        out_specs = o_spec
        out_shape = o_shape
        scratch = [pltpu.VMEM((tm, k), BF16)]
    return pl.pallas_call(
        _norm_matmul_kernel,
        grid=(m // tm, n // tn),
        in_specs=in_specs,
        out_specs=out_specs,
        out_shape=out_shape,
        scratch_shapes=scratch,
        compiler_params=_params("parallel", "arbitrary"),
        name="norm_matmul",
    )(x, g.reshape(1, k).astype(F32), w)


def _matmul_res_kernel(a_ref, w_ref, r_ref, o_ref):
    o_ref[...] = r_ref[...] + jnp.dot(a_ref[...], w_ref[...], preferred_element_type=F32)


def _matmul_res(a, w, res, *, tm, tn):
    m, k = a.shape
    n = w.shape[1]
    assert m % tm == 0 and n % tn == 0
    return pl.pallas_call(
        _matmul_res_kernel,
        grid=(m // tm, n // tn),
        in_specs=[
            pl.BlockSpec((tm, k), lambda i, j: (i, 0)),
            pl.BlockSpec((k, tn), lambda i, j: (0, j)),
            pl.BlockSpec((tm, tn), lambda i, j: (i, j)),
        ],
        out_specs=pl.BlockSpec((tm, tn), lambda i, j: (i, j)),
        out_shape=jax.ShapeDtypeStruct((m, n), F32),
        compiler_params=_params("parallel", "parallel"),
        name="matmul_res",
    )(a, w, res)


def _tables_kernel(pos_ref, inv_ref, rc_ref, rs_ref, tc_ref, ts_ref, *, rows):
    s = pos_ref.shape[1]
    lane = lax.broadcasted_iota(jnp.int32, (1, HEAD), 1)
    half_sign = jnp.where(lane < HEAD // 2, -1.0, 1.0).astype(F32)
    pair_sign = jnp.where(lane % 2 == 0, -1.0, 1.0).astype(F32)

    def body(c, _):
        sl = pl.ds(pl.multiple_of(c * rows, rows), rows)
        pos = pos_ref[0, sl, :]
        ang = pos * inv_ref[0:1, :]
        rc_ref[0, sl, :] = jnp.cos(ang)
        rs_ref[0, sl, :] = jnp.sin(ang) * half_sign
        ang = pos * inv_ref[1:2, :]
        tc_ref[0, sl, :] = jnp.cos(ang)
        ts_ref[0, sl, :] = jnp.sin(ang) * pair_sign
        return 0

    lax.fori_loop(0, s // rows, body, 0)


def _rotation_tables(positions):
    b, s = positions.shape
    half = HEAD // 2
    inv_half = 1.0 / (ROPE_THETA ** (jnp.arange(half, dtype=F32) / half))
    inv_pair = 1.0 / (10000.0 ** jnp.linspace(0.0, 1.0, half, dtype=F32))
    inv = jnp.stack([jnp.concatenate([inv_half, inv_half]), jnp.repeat(inv_pair, 2)])
    pos = positions.astype(F32).reshape(b, s, 1)
    tab = jax.ShapeDtypeStruct((b, s, HEAD), F32)
    spec = pl.BlockSpec((1, s, HEAD), lambda i: (i, 0, 0))
    return pl.pallas_call(
        functools.partial(_tables_kernel, rows=256),
        grid=(b,),
        in_specs=[pl.BlockSpec((1, s, 1), lambda i: (i, 0, 0)),
                  pl.BlockSpec((2, HEAD), lambda i: (0, 0))],
        out_specs=[spec] * 4,
        out_shape=[tab] * 4,
        compiler_params=_params("parallel"),
        name="rotation_tables",
    )(pos, inv)


def _retention_kernel(logg_ref, q_ref, k_ref, v_ref, g_ref, cos_ref, sin_ref, o_ref, state_ref):
    c = RET_CHUNK
    s = q_ref.shape[1]
    lg = logg_ref[pl.program_id(1)]
    row = lax.broadcasted_iota(jnp.int32, (c, c), 0).astype(F32)
    col = lax.broadcasted_iota(jnp.int32, (c, c), 1).astype(F32)
    diff = row - col
    dmat = jnp.where(diff >= 0, jnp.exp(lg * jnp.maximum(diff, 0.0)), 0.0)
    k_w = jnp.exp(lg * (c - 1.0 - row))
    q_w = jnp.exp(lg * (row + 1.0))
    chunk_decay = jnp.exp(lg * jnp.full((c, c), float(c), F32))
    even = lax.broadcasted_iota(jnp.int32, (c, HEAD), 1) % 2 == 0
    scale = HEAD ** -0.5

    state_ref[...] = jnp.zeros_like(state_ref)

    def body(n, _):
        sl = pl.ds(pl.multiple_of(n * c, c), c)
        cos = cos_ref[0, sl, :]
        sin = sin_ref[0, sl, :]

        def rot(x):
            swapped = jnp.where(even, pltpu.roll(x, HEAD - 1, 1), pltpu.roll(x, 1, 1))
            return x * cos + swapped * sin

        q = rot(q_ref[0, sl, :].astype(F32))
        k = rot(k_ref[0, sl, :].astype(F32)) * scale
        v = v_ref[0, sl, :]
        scores = lax.dot_general(q.astype(BF16), k.astype(BF16), NT_DIMS,
                                 preferred_element_type=F32) * dmat
        inner = jnp.dot(scores.astype(BF16), v, preferred_element_type=F32)
        state = state_ref[...]
        cross = jnp.dot((q * q_w).astype(BF16), state.astype(BF16), preferred_element_type=F32)
        kv = lax.dot_general((k * k_w).astype(BF16), v, TN_DIMS, preferred_element_type=F32)
        state_ref[...] = chunk_decay * state + kv
        out = inner + cross
        out = out * lax.rsqrt(jnp.mean(out * out, axis=-1, keepdims=True) + NORM_EPS)
        g = g_ref[0, sl, :].astype(F32)
        o_ref[0, sl, :] = (out * (g * jax.nn.sigmoid(g))).astype(o_ref.dtype)
        return 0

    lax.fori_loop(0, s // c, body, 0)


def _retention(proj, ret_cos, ret_sin):
    b, s, _ = proj.shape
    h = RET_HEADS
    log_g = jnp.log1p(-jnp.exp2(-5.0 - jnp.arange(h, dtype=F32)))
    head = lambda off: pl.BlockSpec((1, s, HEAD), lambda i, j: (i, 0, off + j))
    tab = pl.BlockSpec((1, s, HEAD), lambda i, j: (i, 0, 0))
    return pl.pallas_call(
        _retention_kernel,
        grid=(b, h),
        in_specs=[pl.BlockSpec(memory_space=pltpu.SMEM),
                  head(0), head(h), head(2 * h), head(3 * h), tab, tab],
        out_specs=pl.BlockSpec((1, s, HEAD), lambda i, j: (i, 0, j)),
        out_shape=jax.ShapeDtypeStruct((b, s, h * HEAD), BF16),
        scratch_shapes=[pltpu.VMEM((HEAD, HEAD), F32)],
        compiler_params=_params("parallel", "parallel"),
        name="retention",
    )(log_g, proj, proj, proj, proj, ret_cos, ret_sin)


def _dilated_kernel(*refs):
    qkv_refs = refs[:9]
    cos_ref, sin_ref, o_ref, qs, ks, vs, og, ol = refs[9:]
    s = o_ref.shape[1]
    blk = DIL_BLOCK
    scale = HEAD ** -0.5
    neg = -1e30
    qi = lax.broadcasted_iota(jnp.int32, (blk, blk), 0)
    kj = lax.broadcasted_iota(jnp.int32, (blk, blk), 1)
    cos = cos_ref[0]
    sin = sin_ref[0]

    def rot(x):
        return x * cos + pltpu.roll(x, HEAD // 2, 1) * sin

    for g, (window, r) in enumerate(DIL_PATTERNS):
        win = window // r
        length = s // r
        nb = length // blk
        cur_ok = (qi - kj >= 0) & (qi - kj <= win)
        prev_ok = (qi - kj + blk) <= win
        q_ref, k_ref, v_ref = qkv_refs[3 * g:3 * g + 3]
        qs[...] = rot(q_ref[0].astype(F32))
        ks[...] = rot(k_ref[0].astype(F32))
        vs[...] = v_ref[0].astype(F32)
        for c in range(r):
            for n in range(nb):
                rows = lambda m: pl.ds(c + m * blk * r, blk, stride=r) if r > 1 else pl.ds(m * blk, blk)
                q = qs[rows(n), :].astype(BF16)
                k1 = ks[rows(n), :].astype(BF16)
                v1 = vs[rows(n), :].astype(BF16)
                s1 = lax.dot_general(q, k1, NT_DIMS, preferred_element_type=F32) * scale
                s1 = jnp.where(cur_ok, s1, neg)
                m = jnp.max(s1, axis=-1, keepdims=True)
                if n > 0:
                    k0 = ks[rows(n - 1), :].astype(BF16)
                    v0 = vs[rows(n - 1), :].astype(BF16)
                    s0 = lax.dot_general(q, k0, NT_DIMS, preferred_element_type=F32) * scale
                    s0 = jnp.where(prev_ok, s0, neg)
                    m = jnp.maximum(m, jnp.max(s0, axis=-1, keepdims=True))
                p1 = jnp.exp(s1 - m)
                l = jnp.sum(p1, axis=-1, keepdims=True)
                acc = jnp.dot(p1.astype(BF16), v1, preferred_element_type=F32)
                if n > 0:
                    p0 = jnp.exp(s0 - m)
                    l = l + jnp.sum(p0, axis=-1, keepdims=True)
                    acc = acc + jnp.dot(p0.astype(BF16), v0, preferred_element_type=F32)
                og[g, rows(n), :] = acc / l
                ol[g, rows(n), :] = jnp.broadcast_to(m + jnp.log(l), (blk, HEAD))

    ng = len(DIL_PATTERNS)
    mx = ol[0]
    for g in range(1, ng):
        mx = jnp.maximum(mx, ol[g])
    num = jnp.zeros((s, HEAD), F32)
    den = jnp.zeros((s, HEAD), F32)
    for g in range(ng):
        e = jnp.exp(ol[g] - mx)
        num = num + e * og[g]
        den = den + e
    o_ref[0] = (num / den).astype(o_ref.dtype)


def _dilated(proj, rot_cos, rot_sin, col0):
    b, s, _ = proj.shape
    assert s % (DIL_BLOCK * max(r for _, r in DIL_PATTERNS)) == 0
    hg = DIL_HEADS_PER_GROUP
    ng = len(DIL_PATTERNS)
    width = ng * hg
    base = col0 // HEAD
    specs = []
    for g in range(ng):
        for part in range(3):
            off = base + part * width + g * hg
            specs.append(pl.BlockSpec((1, s, HEAD), lambda i, j, off=off: (i, 0, off + j)))
    tab = pl.BlockSpec((1, s, HEAD), lambda i, j: (i, 0, 0))
    return pl.pallas_call(
        _dilated_kernel,
        grid=(b, hg),
        in_specs=specs + [tab, tab],
        out_specs=pl.BlockSpec((1, s, HEAD), lambda i, j: (i, 0, j)),
        out_shape=jax.ShapeDtypeStruct((b, s, hg * HEAD), BF16),
        scratch_shapes=[pltpu.VMEM((s, HEAD), F32)] * 3
        + [pltpu.VMEM((ng, s, HEAD), F32)] * 2,
        compiler_params=_params("parallel", "parallel"),
        name="dilated",
    )(*([proj] * 9), rot_cos, rot_sin)


def _merge_kernel(ret_ref, dil_ref, wr_ref, wd_ref, gr_ref, gd_ref, o_ref):
    a = jnp.dot(ret_ref[...], wr_ref[...], preferred_element_type=F32)
    d = jnp.dot(dil_ref[...], wd_ref[...], preferred_element_type=F32)
    gr = jax.nn.sigmoid(gr_ref[...].astype(F32))
    gd = jax.nn.sigmoid(gd_ref[...].astype(F32))
    o_ref[...] = (gr * a + gd * d).astype(o_ref.dtype)


def _merge(ret, dil, w_br_ret, w_br_dil, proj2d, gate_col0, *, tm, tn):
    m, d_model = ret.shape[0], w_br_ret.shape[1]
    assert gate_col0 % tn == 0 and d_model % tn == 0 and m % tm == 0
    gr0 = gate_col0 // tn
    gd0 = gr0 + d_model // tn
    return pl.pallas_call(
        _merge_kernel,
        grid=(m // tm, d_model // tn),
        in_specs=[
            pl.BlockSpec((tm, ret.shape[1]), lambda i, j: (i, 0)),
            pl.BlockSpec((tm, dil.shape[1]), lambda i, j: (i, 0)),
            pl.BlockSpec((w_br_ret.shape[0], tn), lambda i, j: (0, j)),
            pl.BlockSpec((w_br_dil.shape[0], tn), lambda i, j: (0, j)),
            pl.BlockSpec((tm, tn), lambda i, j: (i, gr0 + j)),
            pl.BlockSpec((tm, tn), lambda i, j: (i, gd0 + j)),
        ],
        out_specs=pl.BlockSpec((tm, tn), lambda i, j: (i, j)),
        out_shape=jax.ShapeDtypeStruct((m, d_model), BF16),
        compiler_params=_params("parallel", "parallel"),
        name="merge",
    )(ret, dil, w_br_ret, w_br_dil, proj2d, proj2d)


def _mem_attn_kernel(q_ref, k_ref, v_ref, o_ref):
    dh = q_ref.shape[2]
    sc = lax.dot_general(q_ref[0], k_ref[0], NT_DIMS, preferred_element_type=F32) * (dh ** -0.5)
    m = jnp.max(sc, axis=-1, keepdims=True)
    p = jnp.exp(sc - m)
    l = jnp.sum(p, axis=-1, keepdims=True)
    o = jnp.dot(p.astype(BF16), v_ref[0], preferred_element_type=F32)
    o_ref[0] = (o / l).astype(o_ref.dtype)


def _mem_attention(q, kv, *, ts):
    b, s, d = q.shape
    m = kv.shape[1]
    dh = d // MEM_HEADS
    return pl.pallas_call(
        _mem_attn_kernel,
        grid=(b, s // ts, MEM_HEADS),
        in_specs=[
            pl.BlockSpec((1, ts, dh), lambda i, j, h: (i, j, h)),
            pl.BlockSpec((1, m, dh), lambda i, j, h: (i, 0, h)),
            pl.BlockSpec((1, m, dh), lambda i, j, h: (i, 0, MEM_HEADS + h)),
        ],
        out_specs=pl.BlockSpec((1, ts, dh), lambda i, j, h: (i, j, h)),
        out_shape=jax.ShapeDtypeStruct((b, s, d), BF16),
        compiler_params=_params("parallel", "parallel", "parallel"),
        name="mem_attention",
    )(q, kv, kv)


def _extract_top(w, count):
    n = w.shape[0]
    idx = lax.broadcasted_iota(jnp.int32, w.shape, 0).astype(F32)
    vals = []
    for _ in range(count):
        m = jnp.max(w, axis=0, keepdims=True)
        first = jnp.min(jnp.where(w == m, idx, float(n)), axis=0, keepdims=True)
        w = jnp.where(idx == first, -jnp.inf, w)
        vals.append(m)
    return vals, w


def _peer_topk_kernel(q_ref, sk_ref, rho_ref, e1_ref, e2_ref):
    k = PEER_TOPK
    sc1 = lax.dot_general(sk_ref[0, 0], q_ref[:, :HEAD], NT_DIMS, preferred_element_type=F32)
    sc2 = lax.dot_general(sk_ref[0, 1], q_ref[:, HEAD:], NT_DIMS, preferred_element_type=F32)
    v1, left1 = _extract_top(sc1, k)
    v2, left2 = _extract_top(sc2, k)
    sel1 = left1 < sc1
    sel2 = left2 < sc2
    v2_all = jnp.concatenate(v2, axis=0)
    combo = jnp.concatenate([v1[a] + v2_all for a in range(k)], axis=0)
    cv, _ = _extract_top(combo, k + 1)
    z = jnp.zeros_like(cv[0])
    for i in range(k):
        z = z + jnp.exp(cv[i] - cv[0])
    thr = 0.5 * (cv[k - 1] + cv[k])
    m1, m2 = v1[0], v2[0]
    e2_ref[0] = jnp.where(sel2, jnp.exp(sc2 - m2), 0.0)
    e1_ref[0] = jnp.where(sel1, jnp.exp(sc1 - m1), 0.0) / z
    rho_ref[0] = jnp.where(sel1, jnp.exp((thr - m2) - sc1), jnp.inf)


def _peer_topk(qp, subkeys, *, tk):
    t = qp.shape[0]
    h, _, nk, dq = subkeys.shape
    out = jax.ShapeDtypeStruct((h, nk, t), F32)
    spec = pl.BlockSpec((1, nk, tk), lambda i, j: (j, 0, i))
    return pl.pallas_call(
        _peer_topk_kernel,
        grid=(t // tk, h),
        in_specs=[pl.BlockSpec((tk, 2 * dq), lambda i, j: (i, j)),
                  pl.BlockSpec((1, 2, nk, dq), lambda i, j: (j, 0, 0, 0))],
        out_specs=[spec] * 3,
        out_shape=[out] * 3,
        compiler_params=_params("parallel", "parallel"),
        name="peer_topk",
    )(qp, subkeys)


def _peer_ffn_kernel(xn_ref, u_ref, v_ref, rho_ref, e1_ref, e2_ref, res_ref, gf_ref, o_ref, w_ref,
                     *, final_norm):
    j = pl.program_id(1)
    ne = u_ref.shape[0]
    nk = PEER_N_KEYS

    @pl.when(j == 0)
    def _():
        o_ref[...] = jnp.zeros_like(o_ref)

    a = lax.dot_general(u_ref[...], xn_ref[...], NT_DIMS, preferred_element_type=F32)
    for b in range(ne // nk):
        i1 = j * (ne // nk) + b
        gate = jnp.zeros((nk, a.shape[1]), F32)
        for h in range(PEER_HEADS):
            rho = rho_ref[h, pl.ds(i1, 1), :]
            e1 = e1_ref[h, pl.ds(i1, 1), :]
            e2 = e2_ref[h]
            gate = gate + jnp.where(e2 > rho, e2, 0.0) * e1
        ab = a[b * nk:(b + 1) * nk, :]
        act = 0.5 * ab * (1.0 + lax.erf(ab * (0.5 ** 0.5)))
        w_ref[b * nk:(b + 1) * nk, :] = (act * gate).astype(BF16)
    o_ref[...] += lax.dot_general(w_ref[...], v_ref[...], TN_DIMS, preferred_element_type=F32)

    @pl.when(j == pl.num_programs(1) - 1)
    def _():
        y = res_ref[...] + o_ref[...]
        if final_norm:
            ms = jnp.mean(y * y, axis=-1, keepdims=True)
            y = (y * lax.rsqrt(ms + NORM_EPS)) * gf_ref[...]
        o_ref[...] = y


def _peer_ffn(xn, u, v, rho, e1, e2, res, g_final, *, tt, ne, final_norm):
    t, d = xn.shape
    n_exp = u.shape[0]
    h, nk, _ = rho.shape
    assert n_exp == nk * nk and ne % nk == 0 and t % tt == 0 and n_exp % ne == 0
    tok = pl.BlockSpec((h, nk, tt), lambda i, j: (0, 0, i))
    return pl.pallas_call(
        functools.partial(_peer_ffn_kernel, final_norm=final_norm),
        grid=(t // tt, n_exp // ne),
        in_specs=[
            pl.BlockSpec((tt, d), lambda i, j: (i, 0)),
            pl.BlockSpec((ne, d), lambda i, j: (j, 0)),
            pl.BlockSpec((ne, d), lambda i, j: (j, 0)),
            tok, tok, tok,
            pl.BlockSpec((tt, d), lambda i, j: (i, 0)),
            pl.BlockSpec((1, d), lambda i, j: (0, 0)),
        ],
        out_specs=pl.BlockSpec((tt, d), lambda i, j: (i, 0)),
        out_shape=jax.ShapeDtypeStruct((t, d), F32),
        scratch_shapes=[pltpu.VMEM((ne, tt), BF16)],
        compiler_params=_params("parallel", "arbitrary"),
        name="peer_ffn",
    )(xn, u, v, rho, e1, e2, res, g_final.reshape(1, d).astype(F32))


def kernel(x, mem, positions, g_mix, w_in, w_br_ret, w_br_dil, w_out, g_cross, g_mem, w_q_mem,
           w_kv_mem, w_o_mem, g_ffn, w_peer_q, peer_subkeys, peer_u, peer_v, g_final):
    b, s, d = x.shape
    t = b * s
    depth = w_in.shape[0]
    ret_w = RET_HEADS * HEAD
    dil_col0 = 4 * ret_w
    dil_w = len(DIL_PATTERNS) * DIL_HEADS_PER_GROUP * HEAD
    gate_col0 = dil_col0 + 3 * dil_w
    tm = min(1024, t)

    rot_cos, rot_sin, ret_cos, ret_sin = _rotation_tables(positions)
    x2d = x.reshape(t, d)
    for l in range(depth):
        bf = lambda w: w[l].astype(BF16)
        proj = _norm_matmul(x2d, g_mix[l], bf(w_in), tm=tm, tn=1280, out_dtype=BF16)
        proj3 = proj.reshape(b, s, -1)
        ret = _retention(proj3, ret_cos, ret_sin).reshape(t, ret_w)
        dil = _dilated(proj3, rot_cos, rot_sin, dil_col0).reshape(t, -1)
        merged = _merge(ret, dil, bf(w_br_ret), bf(w_br_dil), proj, gate_col0, tm=tm, tn=512)
        x2d = _matmul_res(merged, bf(w_out), x2d, tm=tm, tn=1024)
        mem2d = mem.reshape(-1, d)
        kv = _norm_matmul(mem2d, g_mem[l], bf(w_kv_mem), tm=min(1024, mem2d.shape[0]), tn=1024,
                          out_dtype=BF16)
        qm = _norm_matmul(x2d, g_cross[l], bf(w_q_mem), tm=tm, tn=1024, out_dtype=BF16)
        att = _mem_attention(qm.reshape(b, s, d), kv.reshape(b, -1, 2 * d), ts=min(1024, s))
        x2d = _matmul_res(att.reshape(t, d), bf(w_o_mem), x2d, tm=tm, tn=1024)
        qp, xn = _norm_matmul(x2d, g_ffn[l], bf(w_peer_q), tm=tm, tn=1024, out_dtype=BF16,
                              emit_hn=True)
        rho, e1, e2 = _peer_topk(qp, bf(peer_subkeys), tk=256)
        x2d = _peer_ffn(xn, bf(peer_u), bf(peer_v), rho, e1, e2, x2d, g_final, tt=512, ne=512,
                        final_norm=(l == depth - 1))
    return x2d.reshape(b, s, d)
```

```python
import functools
import math

import jax
import jax.numpy as jnp
import numpy as np
from jax import lax
from jax.experimental import pallas as pl
from jax.experimental.pallas import tpu as pltpu

F32 = jnp.float32
BF16 = jnp.bfloat16

NORM_EPS = 1e-6
ROPE_THETA = 10000.0
HEAD = 128
RET_HEADS = 8
RET_CHUNK = 128
DIL_PATTERNS = ((128, 1), (512, 4), (2048, 16))
DIL_HEADS_PER_GROUP = 4
DIL_BLOCK = 128
MEM_HEADS = 4
PEER_HEADS = 8
PEER_N_KEYS = 128
PEER_TOPK = 16

V7X_VMEM_BYTES = 64 * 1024 * 1024
VMEM_LIMIT = 56 * 1024 * 1024

NT_DIMS = (((1,), (1,)), ((), ()))
TN_DIMS = (((0,), (0,)), ((), ()))


def _params(*sem):
    return pltpu.CompilerParams(dimension_semantics=sem, vmem_limit_bytes=VMEM_LIMIT)


def _norm_rows(x_ref, g_ref, hn_ref, rows):
    tm = x_ref.shape[0]

    def body(c, _):
        sl = pl.ds(pl.multiple_of(c * rows, rows), rows)
        x = x_ref[sl, :].astype(F32)
        ms = jnp.mean(x * x, axis=-1, keepdims=True)
        hn_ref[sl, :] = ((x * lax.rsqrt(ms + NORM_EPS)) * g_ref[...]).astype(BF16)
        return 0

    lax.fori_loop(0, tm // rows, body, 0)


def _norm_matmul_kernel(x_ref, g_ref, w_ref, o_ref, hn_ref):
    @pl.when(pl.program_id(1) == 0)
    def _():
        _norm_rows(x_ref, g_ref, hn_ref, 256)

    o_ref[...] = jnp.dot(hn_ref[...], w_ref[...], preferred_element_type=F32).astype(o_ref.dtype)


def _norm_matmul(x, g, w, *, tm, tn, out_dtype, emit_hn=False):
    m, k = x.shape
    n = w.shape[1]
    assert m % tm == 0 and n % tn == 0
    in_specs = [
        pl.BlockSpec((tm, k), lambda i, j: (i, 0)),
        pl.BlockSpec((1, k), lambda i, j: (0, 0)),
        pl.BlockSpec((k, tn), lambda i, j: (0, j)),
    ]
    o_spec = pl.BlockSpec((tm, tn), lambda i, j: (i, j))
    o_shape = jax.ShapeDtypeStruct((m, n), out_dtype)
    if emit_hn:
        out_specs = [o_spec, pl.BlockSpec((tm, k), lambda i, j: (i, 0))]
        out_shape = [o_shape, jax.ShapeDtypeStruct((m, k), BF16)]
        scratch = []
    else:
        out_specs = o_spec
        out_shape = o_shape
        scratch = [pltpu.VMEM((tm, k), BF16)]
    return pl.pallas_call(
        _norm_matmul_kernel,
        grid=(m // tm, n // tn),
        in_specs=in_specs,
        out_specs=out_specs,
        out_shape=out_shape,
        scratch_shapes=scratch,
        compiler_params=_params("parallel", "arbitrary"),
        name="norm_matmul",
    )(x, g.reshape(1, k).astype(F32), w)


def _matmul_res_kernel(a_ref, w_ref, r_ref, o_ref):
    o_ref[...] = r_ref[...] + jnp.dot(a_ref[...], w_ref[...], preferred_element_type=F32)


def _matmul_res(a, w, res, *, tm, tn):
    m, k = a.shape
    n = w.shape[1]
    assert m % tm == 0 and n % tn == 0
    return pl.pallas_call(
        _matmul_res_kernel,
        grid=(m // tm, n // tn),
        in_specs=[
            pl.BlockSpec((tm, k), lambda i, j: (i, 0)),
            pl.BlockSpec((k, tn), lambda i, j: (0, j)),
            pl.BlockSpec((tm, tn), lambda i, j: (i, j)),
        ],
        out_specs=pl.BlockSpec((tm, tn), lambda i, j: (i, j)),
        out_shape=jax.ShapeDtypeStruct((m, n), F32),
        compiler_params=_params("parallel", "parallel"),
        name="matmul_res",
    )(a, w, res)


def _tables_kernel(pos_ref, inv_ref, rc_ref, rs_ref, tc_ref, ts_ref, *, rows):
    s = pos_ref.shape[1]
    lane = lax.broadcasted_iota(jnp.int32, (1, HEAD), 1)
    half_sign = jnp.where(lane < HEAD // 2, -1.0, 1.0).astype(F32)
    pair_sign = jnp.where(lane % 2 == 0, -1.0, 1.0).astype(F32)

    def body(c, _):
        sl = pl.ds(pl.multiple_of(c * rows, rows), rows)
        pos = pos_ref[0, sl, :]
        ang = pos * inv_ref[0:1, :]
        rc_ref[0, sl, :] = jnp.cos(ang)
        rs_ref[0, sl, :] = jnp.sin(ang) * half_sign
        ang = pos * inv_ref[1:2, :]
        tc_ref[0, sl, :] = jnp.cos(ang)
        ts_ref[0, sl, :] = jnp.sin(ang) * pair_sign
        return 0

    lax.fori_loop(0, s // rows, body, 0)


def _rotation_tables(positions):
    b, s = positions.shape
    half = HEAD // 2
    inv_half = 1.0 / (ROPE_THETA ** (jnp.arange(half, dtype=F32) / half))
    inv_pair = 1.0 / (10000.0 ** jnp.linspace(0.0, 1.0, half, dtype=F32))
    inv = jnp.stack([jnp.concatenate([inv_half, inv_half]), jnp.repeat(inv_pair, 2)])
    pos = positions.astype(F32).reshape(b, s, 1)
    tab = jax.ShapeDtypeStruct((b, s, HEAD), F32)
    spec = pl.BlockSpec((1, s, HEAD), lambda i: (i, 0, 0))
    return pl.pallas_call(
        functools.partial(_tables_kernel, rows=256),
        grid=(b,),
        in_specs=[pl.BlockSpec((1, s, 1), lambda i: (i, 0, 0)),
                  pl.BlockSpec((2, HEAD), lambda i: (0, 0))],
        out_specs=[spec] * 4,
        out_shape=[tab] * 4,
        compiler_params=_params("parallel"),
        name="rotation_tables",
    )(pos, inv)


def _retention_kernel(logg_ref, q_ref, k_ref, v_ref, g_ref, cos_ref, sin_ref, o_ref, state_ref):
    c = RET_CHUNK
    s = q_ref.shape[1]
    lg = logg_ref[pl.program_id(1)]
    row = lax.broadcasted_iota(jnp.int32, (c, c), 0).astype(F32)
    col = lax.broadcasted_iota(jnp.int32, (c, c), 1).astype(F32)
    diff = row - col
    dmat = jnp.where(diff >= 0, jnp.exp(lg * jnp.maximum(diff, 0.0)), 0.0)
    k_w = jnp.exp(lg * (c - 1.0 - row))
    q_w = jnp.exp(lg * (row + 1.0))
    chunk_decay = jnp.exp(lg * jnp.full((c, c), float(c), F32))
    even = lax.broadcasted_iota(jnp.int32, (c, HEAD), 1) % 2 == 0
    scale = HEAD ** -0.5

    state_ref[...] = jnp.zeros_like(state_ref)

    def body(n, _):
        sl = pl.ds(pl.multiple_of(n * c, c), c)
        cos = cos_ref[0, sl, :]
        sin = sin_ref[0, sl, :]

        def rot(x):
            swapped = jnp.where(even, pltpu.roll(x, HEAD - 1, 1), pltpu.roll(x, 1, 1))
            return x * cos + swapped * sin

        q = rot(q_ref[0, sl, :].astype(F32))
        k = rot(k_ref[0, sl, :].astype(F32)) * scale
        v = v_ref[0, sl, :]
        scores = lax.dot_general(q.astype(BF16), k.astype(BF16), NT_DIMS,
                                 preferred_element_type=F32) * dmat
        inner = jnp.dot(scores.astype(BF16), v, preferred_element_type=F32)
        state = state_ref[...]
        cross = jnp.dot((q * q_w).astype(BF16), state.astype(BF16), preferred_element_type=F32)
        kv = lax.dot_general((k * k_w).astype(BF16), v, TN_DIMS, preferred_element_type=F32)
        state_ref[...] = chunk_decay * state + kv
        out = inner + cross
        out = out * lax.rsqrt(jnp.mean(out * out, axis=-1, keepdims=True) + NORM_EPS)
        g = g_ref[0, sl, :].astype(F32)
        o_ref[0, sl, :] = (out * (g * jax.nn.sigmoid(g))).astype(o_ref.dtype)
        return 0

    lax.fori_loop(0, s // c, body, 0)


def _retention(proj, ret_cos, ret_sin):
    b, s, _ = proj.shape
    h = RET_HEADS
    log_g = jnp.log1p(-jnp.exp2(-5.0 - jnp.arange(h, dtype=F32)))
    head = lambda off: pl.BlockSpec((1, s, HEAD), lambda i, j: (i, 0, off + j))
    tab = pl.BlockSpec((1, s, HEAD), lambda i, j: (i, 0, 0))
    return pl.pallas_call(
        _retention_kernel,
        grid=(b, h),
        in_specs=[pl.BlockSpec(memory_space=pltpu.SMEM),
                  head(0), head(h), head(2 * h), head(3 * h), tab, tab],
        out_specs=pl.BlockSpec((1, s, HEAD), lambda i, j: (i, 0, j)),
        out_shape=jax.ShapeDtypeStruct((b, s, h * HEAD), BF16),
        scratch_shapes=[pltpu.VMEM((HEAD, HEAD), F32)],
        compiler_params=_params("parallel", "parallel"),
        name="retention",
    )(log_g, proj, proj, proj, proj, ret_cos, ret_sin)


def _dilated_kernel(*refs):
    qkv_refs = refs[:9]
    cos_ref, sin_ref, o_ref, qs, ks, vs, og, ol = refs[9:]
    s = o_ref.shape[1]
    blk = DIL_BLOCK
    scale = HEAD ** -0.5
    neg = -1e30
    qi = lax.broadcasted_iota(jnp.int32, (blk, blk), 0)
    kj = lax.broadcasted_iota(jnp.int32, (blk, blk), 1)
    cos = cos_ref[0]
    sin = sin_ref[0]

    def rot(x):
        return x * cos + pltpu.roll(x, HEAD // 2, 1) * sin

    for g, (window, r) in enumerate(DIL_PATTERNS):
        win = window // r
        length = s // r
        nb = length // blk
        cur_ok = (qi - kj >= 0) & (qi - kj <= win)
        prev_ok = (qi - kj + blk) <= win
        q_ref, k_ref, v_ref = qkv_refs[3 * g:3 * g + 3]
        qs[...] = rot(q_ref[0].astype(F32))
        ks[...] = rot(k_ref[0].astype(F32))
        vs[...] = v_ref[0].astype(F32)
        for c in range(r):
            for n in range(nb):
                rows = lambda m: pl.ds(c + m * blk * r, blk, stride=r) if r > 1 else pl.ds(m * blk, blk)
                q = qs[rows(n), :].astype(BF16)
                k1 = ks[rows(n), :].astype(BF16)
                v1 = vs[rows(n), :].astype(BF16)
                s1 = lax.dot_general(q, k1, NT_DIMS, preferred_element_type=F32) * scale
                s1 = jnp.where(cur_ok, s1, neg)
                m = jnp.max(s1, axis=-1, keepdims=True)
                if n > 0:
                    k0 = ks[rows(n - 1), :].astype(BF16)
                    v0 = vs[rows(n - 1), :].astype(BF16)
                    s0 = lax.dot_general(q, k0, NT_DIMS, preferred_element_type=F32) * scale
                    s0 = jnp.where(prev_ok, s0, neg)
                    m = jnp.maximum(m, jnp.max(s0, axis=-1, keepdims=True))
                p1 = jnp.exp(s1 - m)
                l = jnp.sum(p1, axis=-1, keepdims=True)
                acc = jnp.dot(p1.astype(BF16), v1, preferred_element_type=F32)
                if n > 0:
                    p0 = jnp.exp(s0 - m)
                    l = l + jnp.sum(p0, axis=-1, keepdims=True)
                    acc = acc + jnp.dot(p0.astype(BF16), v0, preferred_element_type=F32)
                og[g, rows(n), :] = acc / l
                ol[g, rows(n), :] = jnp.broadcast_to(m + jnp.log(l), (blk, HEAD))

    ng = len(DIL_PATTERNS)
    mx = ol[0]
    for g in range(1, ng):
        mx = jnp.maximum(mx, ol[g])
    num = jnp.zeros((s, HEAD), F32)
    den = jnp.zeros((s, HEAD), F32)
    for g in range(ng):
        e = jnp.exp(ol[g] - mx)
        num = num + e * og[g]
        den = den + e
    o_ref[0] = (num / den).astype(o_ref.dtype)


def _dilated(proj, rot_cos, rot_sin, col0):
    b, s, _ = proj.shape
    assert s % (DIL_BLOCK * max(r for _, r in DIL_PATTERNS)) == 0
    hg = DIL_HEADS_PER_GROUP
    ng = len(DIL_PATTERNS)
    width = ng * hg
    base = col0 // HEAD
    specs = []
    for g in range(ng):
        for part in range(3):
            off = base + part * width + g * hg
            specs.append(pl.BlockSpec((1, s, HEAD), lambda i, j, off=off: (i, 0, off + j)))
    tab = pl.BlockSpec((1, s, HEAD), lambda i, j: (i, 0, 0))
    return pl.pallas_call(
        _dilated_kernel,
        grid=(b, hg),
        in_specs=specs + [tab, tab],
        out_specs=pl.BlockSpec((1, s, HEAD), lambda i, j: (i, 0, j)),
        out_shape=jax.ShapeDtypeStruct((b, s, hg * HEAD), BF16),
        scratch_shapes=[pltpu.VMEM((s, HEAD), F32)] * 3
        + [pltpu.VMEM((ng, s, HEAD), F32)] * 2,
        compiler_params=_params("parallel", "parallel"),
        name="dilated",
    )(*([proj] * 9), rot_cos, rot_sin)


def _merge_kernel(ret_ref, dil_ref, wr_ref, wd_ref, gr_ref, gd_ref, o_ref):
    a = jnp.dot(ret_ref[...], wr_ref[...], preferred_element_type=F32)
    d = jnp.dot(dil_ref[...], wd_ref[...], preferred_element_type=F32)
    gr = jax.nn.sigmoid(gr_ref[...].astype(F32))
    gd = jax.nn.sigmoid(gd_ref[...].astype(F32))
    o_ref[...] = (gr * a + gd * d).astype(o_ref.dtype)


def _merge(ret, dil, w_br_ret, w_br_dil, proj2d, gate_col0, *, tm, tn):
    m, d_model = ret.shape[0], w_br_ret.shape[1]
    assert gate_col0 % tn == 0 and d_model % tn == 0 and m % tm == 0
    gr0 = gate_col0 // tn
    gd0 = gr0 + d_model // tn
    return pl.pallas_call(
        _merge_kernel,
        grid=(m // tm, d_model // tn),
        in_specs=[
            pl.BlockSpec((tm, ret.shape[1]), lambda i, j: (i, 0)),
            pl.BlockSpec((tm, dil.shape[1]), lambda i, j: (i, 0)),
            pl.BlockSpec((w_br_ret.shape[0], tn), lambda i, j: (0, j)),
            pl.BlockSpec((w_br_dil.shape[0], tn), lambda i, j: (0, j)),
            pl.BlockSpec((tm, tn), lambda i, j: (i, gr0 + j)),
            pl.BlockSpec((tm, tn), lambda i, j: (i, gd0 + j)),
        ],
        out_specs=pl.BlockSpec((tm, tn), lambda i, j: (i, j)),
        out_shape=jax.ShapeDtypeStruct((m, d_model), BF16),
        compiler_params=_params("parallel", "parallel"),
        name="merge",
    )(ret, dil, w_br_ret, w_br_dil, proj2d, proj2d)


def _mem_attn_kernel(q_ref, k_ref, v_ref, o_ref):
    dh = q_ref.shape[2]
    sc = lax.dot_general(q_ref[0], k_ref[0], NT_DIMS, preferred_element_type=F32) * (dh ** -0.5)
    m = jnp.max(sc, axis=-1, keepdims=True)
    p = jnp.exp(sc - m)
    l = jnp.sum(p, axis=-1, keepdims=True)
    o = jnp.dot(p.astype(BF16), v_ref[0], preferred_element_type=F32)
    o_ref[0] = (o / l).astype(o_ref.dtype)


def _mem_attention(q, kv, *, ts):
    b, s, d = q.shape
    m = kv.shape[1]
    dh = d // MEM_HEADS
    return pl.pallas_call(
        _mem_attn_kernel,
        grid=(b, s // ts, MEM_HEADS),
        in_specs=[
            pl.BlockSpec((1, ts, dh), lambda i, j, h: (i, j, h)),
            pl.BlockSpec((1, m, dh), lambda i, j, h: (i, 0, h)),
            pl.BlockSpec((1, m, dh), lambda i, j, h: (i, 0, MEM_HEADS + h)),
        ],
        out_specs=pl.BlockSpec((1, ts, dh), lambda i, j, h: (i, j, h)),
        out_shape=jax.ShapeDtypeStruct((b, s, d), BF16),
        compiler_params=_params("parallel", "parallel", "parallel"),
        name="mem_attention",
    )(q, kv, kv)


NOT_RANKED = 127.0


def _extract_top(w, count):
    n = w.shape[0]
    idx = lax.broadcasted_iota(jnp.int32, w.shape, 0).astype(F32)
    rank = jnp.full(w.shape, NOT_RANKED, F32)
    vals = []
    for it in range(count):
        m = jnp.max(w, axis=0, keepdims=True)
        first = jnp.min(jnp.where(w == m, idx, float(n)), axis=0, keepdims=True)
        hit = idx == first
        w = jnp.where(hit, -jnp.inf, w)
        rank = jnp.where(hit, float(it), rank)
        vals.append(m)
    return vals, rank, w


_PAIR_CANDIDATES = tuple((a, b) for a in range(PEER_TOPK) for b in range(PEER_TOPK)
                         if (a + 1) * (b + 1) <= PEER_TOPK)


def _peer_topk_kernel(q_ref, sk_ref, r2_ref, e2_ref, n1_ref, e1_ref):
    k = PEER_TOPK
    lanes = 128
    pad_rows = -len(_PAIR_CANDIDATES) % 8

    def column(c, _):
        tok = pl.ds(pl.multiple_of(c * lanes, lanes), lanes)
        sc1 = lax.dot_general(sk_ref[0, 0], q_ref[tok, :HEAD], NT_DIMS, preferred_element_type=F32)
        sc2 = lax.dot_general(sk_ref[0, 1], q_ref[tok, HEAD:], NT_DIMS, preferred_element_type=F32)
        v1, rank1, _ = _extract_top(sc1, k)
        v2, rank2, _ = _extract_top(sc2, k)
        rows = [v1[a] + v2[b] for a, b in _PAIR_CANDIDATES]
        rows += [jnp.full_like(rows[0], -jnp.inf)] * pad_rows
        combo = jnp.concatenate(rows, axis=0)
        cv, crank, _ = _extract_top(combo, k)
        taken = jnp.where(crank < NOT_RANKED, 1.0, 0.0)
        z = jnp.zeros_like(cv[0])
        for i in range(k):
            z = z + jnp.exp(cv[i] - cv[0])
        n1 = jnp.zeros_like(sc1)
        for a in range(k):
            rows_a = [r for r, (ca, _) in enumerate(_PAIR_CANDIDATES) if ca == a]
            cnt = taken[rows_a[0]:rows_a[0] + 1, :]
            for r in rows_a[1:]:
                cnt = cnt + taken[r:r + 1, :]
            n1 = jnp.where(rank1 == float(a), cnt, n1)
        sel1 = rank1 < NOT_RANKED
        sel2 = rank2 < NOT_RANKED
        r2_ref[0, :, tok] = rank2.astype(BF16)
        e2_ref[0, :, tok] = jnp.where(sel2, jnp.exp(sc2 - v2[0]), 0.0).astype(BF16)
        n1_ref[0, :, tok] = n1
        e1_ref[0, :, tok] = jnp.where(sel1, jnp.exp(sc1 - v1[0]), 0.0) / z
        return 0

    lax.fori_loop(0, q_ref.shape[0] // lanes, column, 0)


def _peer_topk(qp, subkeys, *, tk):
    t = qp.shape[0]
    h, _, nk, dq = subkeys.shape
    spec = pl.BlockSpec((1, nk, tk), lambda i, j: (j, 0, i))
    narrow = jax.ShapeDtypeStruct((h, nk, t), BF16)
    wide = jax.ShapeDtypeStruct((h, nk, t), F32)
    return pl.pallas_call(
        _peer_topk_kernel,
        grid=(t // tk, h),
        in_specs=[pl.BlockSpec((tk, 2 * dq), lambda i, j: (i, j)),
                  pl.BlockSpec((1, 2, nk, dq), lambda i, j: (j, 0, 0, 0))],
        out_specs=[spec] * 4,
        out_shape=[narrow, narrow, wide, wide],
        compiler_params=_params("parallel", "parallel"),
        name="peer_topk",
    )(qp, subkeys)


BF16_SUBLANES = 16


def _row_to_bf16_tile(row, rows):
    tile = jnp.broadcast_to(row, (BF16_SUBLANES, row.shape[1])).astype(BF16)
    return jnp.concatenate([tile] * (rows // BF16_SUBLANES), axis=0)


def _peer_ffn_kernel(xn_ref, u_ref, v_ref, r2_ref, e2_ref, n1_ref, e1_ref, res_ref, gf_ref, o_ref,
                     w_ref, *, final_norm):
    j = pl.program_id(1)
    ne = u_ref.shape[0]
    nk = PEER_N_KEYS

    @pl.when(j == 0)
    def _():
        o_ref[...] = jnp.zeros_like(o_ref)

    a = lax.dot_general(u_ref[...], xn_ref[...], NT_DIMS, preferred_element_type=F32)
    for b in range(ne // nk):
        gate = None
        for h in range(PEER_HEADS):
            n1 = _row_to_bf16_tile(n1_ref[h, b:b + 1, :], nk)
            e1 = _row_to_bf16_tile(e1_ref[h, b:b + 1, :], nk)
            term = jnp.where(r2_ref[h] < n1, e2_ref[h], jnp.zeros((), BF16)) * e1
            gate = term if gate is None else gate + term
        ab = a[b * nk:(b + 1) * nk, :]
        act = 0.5 * ab * (1.0 + lax.erf(ab * (0.5 ** 0.5)))
        w_ref[b * nk:(b + 1) * nk, :] = act.astype(BF16) * gate
    o_ref[...] += lax.dot_general(w_ref[...], v_ref[...], TN_DIMS, preferred_element_type=F32)

    @pl.when(j == pl.num_programs(1) - 1)
    def _():
        y = res_ref[...] + o_ref[...]
        if final_norm:
            ms = jnp.mean(y * y, axis=-1, keepdims=True)
            y = (y * lax.rsqrt(ms + NORM_EPS)) * gf_ref[...]
        o_ref[...] = y


def _peer_ffn(xn, u, v, r2, e2, n1, e1, res, g_final, *, tt, ne, final_norm):
    t, d = xn.shape
    n_exp = u.shape[0]
    h, nk, _ = r2.shape
    rows = ne // nk
    assert n_exp == nk * nk and ne % nk == 0 and t % tt == 0 and n_exp % ne == 0 and rows % 8 == 0
    keys = pl.BlockSpec((h, nk, tt), lambda i, j: (0, 0, i))
    first = pl.BlockSpec((h, rows, tt), lambda i, j: (0, j, i))
    return pl.pallas_call(
        functools.partial(_peer_ffn_kernel, final_norm=final_norm),
        grid=(t // tt, n_exp // ne),
        in_specs=[
            pl.BlockSpec((tt, d), lambda i, j: (i, 0)),
            pl.BlockSpec((ne, d), lambda i, j: (j, 0)),
            pl.BlockSpec((ne, d), lambda i, j: (j, 0)),
            keys, keys, first, first,
            pl.BlockSpec((tt, d), lambda i, j: (i, 0)),
            pl.BlockSpec((1, d), lambda i, j: (0, 0)),
        ],
        out_specs=pl.BlockSpec((tt, d), lambda i, j: (i, 0)),
        out_shape=jax.ShapeDtypeStruct((t, d), F32),
        scratch_shapes=[pltpu.VMEM((ne, tt), BF16)],
        compiler_params=_params("parallel", "arbitrary"),
        name="peer_ffn",
    )(xn, u, v, r2, e2, n1, e1, res, g_final.reshape(1, d).astype(F32))


def kernel(x, mem, positions, g_mix, w_in, w_br_ret, w_br_dil, w_out, g_cross, g_mem, w_q_mem,
           w_kv_mem, w_o_mem, g_ffn, w_peer_q, peer_subkeys, peer_u, peer_v, g_final):
    b, s, d = x.shape
    t = b * s
    depth = w_in.shape[0]
    ret_w = RET_HEADS * HEAD
    dil_col0 = 4 * ret_w
    dil_w = len(DIL_PATTERNS) * DIL_HEADS_PER_GROUP * HEAD
    gate_col0 = dil_col0 + 3 * dil_w
    tm = min(1024, t)

    rot_cos, rot_sin, ret_cos, ret_sin = _rotation_tables(positions)
    x2d = x.reshape(t, d)
    for l in range(depth):
        bf = lambda w: w[l].astype(BF16)
        proj = _norm_matmul(x2d, g_mix[l], bf(w_in), tm=tm, tn=1280, out_dtype=BF16)
        proj3 = proj.reshape(b, s, -1)
        ret = _retention(proj3, ret_cos, ret_sin).reshape(t, ret_w)
        dil = _dilated(proj3, rot_cos, rot_sin, dil_col0).reshape(t, -1)
        merged = _merge(ret, dil, bf(w_br_ret), bf(w_br_dil), proj, gate_col0, tm=tm, tn=512)
        x2d = _matmul_res(merged, bf(w_out), x2d, tm=tm, tn=1024)
        mem2d = mem.reshape(-1, d)
        kv = _norm_matmul(mem2d, g_mem[l], bf(w_kv_mem), tm=min(1024, mem2d.shape[0]), tn=1024,
                          out_dtype=BF16)
        qm = _norm_matmul(x2d, g_cross[l], bf(w_q_mem), tm=tm, tn=1024, out_dtype=BF16)
        att = _mem_attention(qm.reshape(b, s, d), kv.reshape(b, -1, 2 * d), ts=min(1024, s))
        x2d = _matmul_res(att.reshape(t, d), bf(w_o_mem), x2d, tm=tm, tn=1024)
        qp, xn = _norm_matmul(x2d, g_ffn[l], bf(w_peer_q), tm=tm, tn=1024, out_dtype=BF16,
                              emit_hn=True)
        r2, e2, n1, e1 = _peer_topk(qp, bf(peer_subkeys), tk=512)
        x2d = _peer_ffn(xn, bf(peer_u), bf(peer_v), r2, e2, n1, e1, x2d, g_final, tt=512, ne=1024,
                        final_norm=(l == depth - 1))
    return x2d.reshape(b, s, d)
```

```python
import functools
import math

import jax
import jax.numpy as jnp
import numpy as np
from jax import lax
from jax.experimental import pallas as pl
from jax.experimental.pallas import tpu as pltpu

F32 = jnp.float32
BF16 = jnp.bfloat16

NORM_EPS = 1e-6
ROPE_THETA = 10000.0
HEAD = 128
RET_HEADS = 8
RET_CHUNK = 128
DIL_PATTERNS = ((128, 1), (512, 4), (2048, 16))
DIL_HEADS_PER_GROUP = 4
DIL_BLOCK = 128
MEM_HEADS = 4
PEER_HEADS = 8
PEER_N_KEYS = 128
PEER_TOPK = 16

V7X_VMEM_BYTES = 64 * 1024 * 1024
VMEM_LIMIT = 56 * 1024 * 1024

NT_DIMS = (((1,), (1,)), ((), ()))
TN_DIMS = (((0,), (0,)), ((), ()))


def _params(*sem):
    return pltpu.CompilerParams(dimension_semantics=sem, vmem_limit_bytes=VMEM_LIMIT)


def _norm_rows(x_ref, g_ref, hn_ref, rows):
    tm = x_ref.shape[0]

    def body(c, _):
        sl = pl.ds(pl.multiple_of(c * rows, rows), rows)
        x = x_ref[sl, :].astype(F32)
        ms = jnp.mean(x * x, axis=-1, keepdims=True)
        hn_ref[sl, :] = ((x * lax.rsqrt(ms + NORM_EPS)) * g_ref[...]).astype(BF16)
        return 0

    lax.fori_loop(0, tm // rows, body, 0)


def _norm_matmul_kernel(x_ref, g_ref, w_ref, o_ref, hn_ref):
    @pl.when(pl.program_id(1) == 0)
    def _():
        _norm_rows(x_ref, g_ref, hn_ref, 256)

    o_ref[...] = jnp.dot(hn_ref[...], w_ref[...], preferred_element_type=F32).astype(o_ref.dtype)


def _norm_matmul(x, g, w, *, tm, tn, out_dtype, emit_hn=False):
    m, k = x.shape
    n = w.shape[1]
    assert m % tm == 0 and n % tn == 0
    in_specs = [
        pl.BlockSpec((tm, k), lambda i, j: (i, 0)),
        pl.BlockSpec((1, k), lambda i, j: (0, 0)),
        pl.BlockSpec((k, tn), lambda i, j: (0, j)),
    ]
    o_spec = pl.BlockSpec((tm, tn), lambda i, j: (i, j))
    o_shape = jax.ShapeDtypeStruct((m, n), out_dtype)
    if emit_hn:
        out_specs = [o_spec, pl.BlockSpec((tm, k), lambda i, j: (i, 0))]
        out_shape = [o_shape, jax.ShapeDtypeStruct((m, k), BF16)]
        scratch = []
    else:
        out_specs = o_spec
        out_shape = o_shape
        scratch = [pltpu.VMEM((tm, k), BF16)]
    return pl.pallas_call(
        _norm_matmul_kernel,
        grid=(m // tm, n // tn),
        in_specs=in_specs,
        out_specs=out_specs,
        out_shape=out_shape,
        scratch_shapes=scratch,
        compiler_params=_params("parallel", "arbitrary"),
        name="norm_matmul",
    )(x, g.reshape(1, k).astype(F32), w)


def _matmul_res_kernel(a_ref, w_ref, r_ref, o_ref):
    o_ref[...] = r_ref[...] + jnp.dot(a_ref[...], w_ref[...], preferred_element_type=F32)


def _matmul_res(a, w, res, *, tm, tn):
    m, k = a.shape
    n = w.shape[1]
    assert m % tm == 0 and n % tn == 0
    return pl.pallas_call(
        _matmul_res_kernel,
        grid=(m // tm, n // tn),
        in_specs=[
            pl.BlockSpec((tm, k), lambda i, j: (i, 0)),
            pl.BlockSpec((k, tn), lambda i, j: (0, j)),
            pl.BlockSpec((tm, tn), lambda i, j: (i, j)),
        ],
        out_specs=pl.BlockSpec((tm, tn), lambda i, j: (i, j)),
        out_shape=jax.ShapeDtypeStruct((m, n), F32),
        compiler_params=_params("parallel", "parallel"),
        name="matmul_res",
    )(a, w, res)


def _tables_kernel(pos_ref, inv_ref, rc_ref, rs_ref, tc_ref, ts_ref, *, rows):
    s = pos_ref.shape[1]
    lane = lax.broadcasted_iota(jnp.int32, (1, HEAD), 1)
    half_sign = jnp.where(lane < HEAD // 2, -1.0, 1.0).astype(F32)
    pair_sign = jnp.where(lane % 2 == 0, -1.0, 1.0).astype(F32)

    def body(c, _):
        sl = pl.ds(pl.multiple_of(c * rows, rows), rows)
        pos = pos_ref[0, sl, :]
        ang = pos * inv_ref[0:1, :]
        rc_ref[0, sl, :] = jnp.cos(ang)
        rs_ref[0, sl, :] = jnp.sin(ang) * half_sign
        ang = pos * inv_ref[1:2, :]
        tc_ref[0, sl, :] = jnp.cos(ang)
        ts_ref[0, sl, :] = jnp.sin(ang) * pair_sign
        return 0

    lax.fori_loop(0, s // rows, body, 0)


def _rotation_tables(positions):
    b, s = positions.shape
    half = HEAD // 2
    inv_half = 1.0 / (ROPE_THETA ** (jnp.arange(half, dtype=F32) / half))
    inv_pair = 1.0 / (10000.0 ** jnp.linspace(0.0, 1.0, half, dtype=F32))
    inv = jnp.stack([jnp.concatenate([inv_half, inv_half]), jnp.repeat(inv_pair, 2)])
    pos = positions.astype(F32).reshape(b, s, 1)
    tab = jax.ShapeDtypeStruct((b, s, HEAD), F32)
    spec = pl.BlockSpec((1, s, HEAD), lambda i: (i, 0, 0))
    return pl.pallas_call(
        functools.partial(_tables_kernel, rows=256),
        grid=(b,),
        in_specs=[pl.BlockSpec((1, s, 1), lambda i: (i, 0, 0)),
                  pl.BlockSpec((2, HEAD), lambda i: (0, 0))],
        out_specs=[spec] * 4,
        out_shape=[tab] * 4,
        compiler_params=_params("parallel"),
        name="rotation_tables",
    )(pos, inv)


def _retention_kernel(logg_ref, q_ref, k_ref, v_ref, g_ref, cos_ref, sin_ref, o_ref, state_ref):
    c = RET_CHUNK
    s = q_ref.shape[1]
    lg = logg_ref[pl.program_id(1)]
    row = lax.broadcasted_iota(jnp.int32, (c, c), 0).astype(F32)
    col = lax.broadcasted_iota(jnp.int32, (c, c), 1).astype(F32)
    diff = row - col
    dmat = jnp.where(diff >= 0, jnp.exp(lg * jnp.maximum(diff, 0.0)), 0.0)
    k_w = jnp.exp(lg * (c - 1.0 - row))
    q_w = jnp.exp(lg * (row + 1.0))
    chunk_decay = jnp.exp(lg * jnp.full((c, c), float(c), F32))
    even = lax.broadcasted_iota(jnp.int32, (c, HEAD), 1) % 2 == 0
    scale = HEAD ** -0.5

    state_ref[...] = jnp.zeros_like(state_ref)

    def body(n, _):
        sl = pl.ds(pl.multiple_of(n * c, c), c)
        cos = cos_ref[0, sl, :]
        sin = sin_ref[0, sl, :]

        def rot(x):
            swapped = jnp.where(even, pltpu.roll(x, HEAD - 1, 1), pltpu.roll(x, 1, 1))
            return x * cos + swapped * sin

        q = rot(q_ref[0, sl, :].astype(F32))
        k = rot(k_ref[0, sl, :].astype(F32)) * scale
        v = v_ref[0, sl, :]
        scores = lax.dot_general(q.astype(BF16), k.astype(BF16), NT_DIMS,
                                 preferred_element_type=F32) * dmat
        inner = jnp.dot(scores.astype(BF16), v, preferred_element_type=F32)
        state = state_ref[...]
        cross = jnp.dot((q * q_w).astype(BF16), state.astype(BF16), preferred_element_type=F32)
        kv = lax.dot_general((k * k_w).astype(BF16), v, TN_DIMS, preferred_element_type=F32)
        state_ref[...] = chunk_decay * state + kv
        out = inner + cross
        out = out * lax.rsqrt(jnp.mean(out * out, axis=-1, keepdims=True) + NORM_EPS)
        g = g_ref[0, sl, :].astype(F32)
        o_ref[0, sl, :] = (out * (g * jax.nn.sigmoid(g))).astype(o_ref.dtype)
        return 0

    lax.fori_loop(0, s // c, body, 0)


def _retention(proj, ret_cos, ret_sin):
    b, s, _ = proj.shape
    h = RET_HEADS
    log_g = jnp.log1p(-jnp.exp2(-5.0 - jnp.arange(h, dtype=F32)))
    head = lambda off: pl.BlockSpec((1, s, HEAD), lambda i, j: (i, 0, off + j))
    tab = pl.BlockSpec((1, s, HEAD), lambda i, j: (i, 0, 0))
    return pl.pallas_call(
        _retention_kernel,
        grid=(b, h),
        in_specs=[pl.BlockSpec(memory_space=pltpu.SMEM),
                  head(0), head(h), head(2 * h), head(3 * h), tab, tab],
        out_specs=pl.BlockSpec((1, s, HEAD), lambda i, j: (i, 0, j)),
        out_shape=jax.ShapeDtypeStruct((b, s, h * HEAD), BF16),
        scratch_shapes=[pltpu.VMEM((HEAD, HEAD), F32)],
        compiler_params=_params("parallel", "parallel"),
        name="retention",
    )(log_g, proj, proj, proj, proj, ret_cos, ret_sin)


def _dilated_kernel(*refs):
    qkv_refs = refs[:9]
    cos_ref, sin_ref, o_ref, qs, ks, vs, og, ol = refs[9:]
    s = o_ref.shape[1]
    blk = DIL_BLOCK
    scale = HEAD ** -0.5
    neg = -1e30
    qi = lax.broadcasted_iota(jnp.int32, (blk, blk), 0)
    kj = lax.broadcasted_iota(jnp.int32, (blk, blk), 1)
    cos = cos_ref[0]
    sin = sin_ref[0]

    def rot(x):
        return x * cos + pltpu.roll(x, HEAD // 2, 1) * sin

    for g, (window, r) in enumerate(DIL_PATTERNS):
        win = window // r
        length = s // r
        nb = length // blk
        cur_ok = (qi - kj >= 0) & (qi - kj <= win)
        prev_ok = (qi - kj + blk) <= win
        q_ref, k_ref, v_ref = qkv_refs[3 * g:3 * g + 3]
        qs[...] = rot(q_ref[0].astype(F32))
        ks[...] = rot(k_ref[0].astype(F32))
        vs[...] = v_ref[0].astype(F32)
        for c in range(r):
            for n in range(nb):
                rows = lambda m: pl.ds(c + m * blk * r, blk, stride=r) if r > 1 else pl.ds(m * blk, blk)
                q = qs[rows(n), :].astype(BF16)
                k1 = ks[rows(n), :].astype(BF16)
                v1 = vs[rows(n), :].astype(BF16)
                s1 = lax.dot_general(q, k1, NT_DIMS, preferred_element_type=F32) * scale
                s1 = jnp.where(cur_ok, s1, neg)
                m = jnp.max(s1, axis=-1, keepdims=True)
                if n > 0:
                    k0 = ks[rows(n - 1), :].astype(BF16)
                    v0 = vs[rows(n - 1), :].astype(BF16)
                    s0 = lax.dot_general(q, k0, NT_DIMS, preferred_element_type=F32) * scale
                    s0 = jnp.where(prev_ok, s0, neg)
                    m = jnp.maximum(m, jnp.max(s0, axis=-1, keepdims=True))
                p1 = jnp.exp(s1 - m)
                l = jnp.sum(p1, axis=-1, keepdims=True)
                acc = jnp.dot(p1.astype(BF16), v1, preferred_element_type=F32)
                if n > 0:
                    p0 = jnp.exp(s0 - m)
                    l = l + jnp.sum(p0, axis=-1, keepdims=True)
                    acc = acc + jnp.dot(p0.astype(BF16), v0, preferred_element_type=F32)
                og[g, rows(n), :] = acc / l
                ol[g, rows(n), :] = jnp.broadcast_to(m + jnp.log(l), (blk, HEAD))

    ng = len(DIL_PATTERNS)
    mx = ol[0]
    for g in range(1, ng):
        mx = jnp.maximum(mx, ol[g])
    num = jnp.zeros((s, HEAD), F32)
    den = jnp.zeros((s, HEAD), F32)
    for g in range(ng):
        e = jnp.exp(ol[g] - mx)
        num = num + e * og[g]
        den = den + e
    o_ref[0] = (num / den).astype(o_ref.dtype)


def _dilated(proj, rot_cos, rot_sin, col0):
    b, s, _ = proj.shape
    assert s % (DIL_BLOCK * max(r for _, r in DIL_PATTERNS)) == 0
    hg = DIL_HEADS_PER_GROUP
    ng = len(DIL_PATTERNS)
    width = ng * hg
    base = col0 // HEAD
    specs = []
    for g in range(ng):
        for part in range(3):
            off = base + part * width + g * hg
            specs.append(pl.BlockSpec((1, s, HEAD), lambda i, j, off=off: (i, 0, off + j)))
    tab = pl.BlockSpec((1, s, HEAD), lambda i, j: (i, 0, 0))
    return pl.pallas_call(
        _dilated_kernel,
        grid=(b, hg),
        in_specs=specs + [tab, tab],
        out_specs=pl.BlockSpec((1, s, HEAD), lambda i, j: (i, 0, j)),
        out_shape=jax.ShapeDtypeStruct((b, s, hg * HEAD), BF16),
        scratch_shapes=[pltpu.VMEM((s, HEAD), F32)] * 3
        + [pltpu.VMEM((ng, s, HEAD), F32)] * 2,
        compiler_params=_params("parallel", "parallel"),
        name="dilated",
    )(*([proj] * 9), rot_cos, rot_sin)


def _merge_kernel(ret_ref, dil_ref, wr_ref, wd_ref, gr_ref, gd_ref, o_ref):
    a = jnp.dot(ret_ref[...], wr_ref[...], preferred_element_type=F32)
    d = jnp.dot(dil_ref[...], wd_ref[...], preferred_element_type=F32)
    gr = jax.nn.sigmoid(gr_ref[...].astype(F32))
    gd = jax.nn.sigmoid(gd_ref[...].astype(F32))
    o_ref[...] = (gr * a + gd * d).astype(o_ref.dtype)


def _merge(ret, dil, w_br_ret, w_br_dil, proj2d, gate_col0, *, tm, tn):
    m, d_model = ret.shape[0], w_br_ret.shape[1]
    assert gate_col0 % tn == 0 and d_model % tn == 0 and m % tm == 0
    gr0 = gate_col0 // tn
    gd0 = gr0 + d_model // tn
    return pl.pallas_call(
        _merge_kernel,
        grid=(m // tm, d_model // tn),
        in_specs=[
            pl.BlockSpec((tm, ret.shape[1]), lambda i, j: (i, 0)),
            pl.BlockSpec((tm, dil.shape[1]), lambda i, j: (i, 0)),
            pl.BlockSpec((w_br_ret.shape[0], tn), lambda i, j: (0, j)),
            pl.BlockSpec((w_br_dil.shape[0], tn), lambda i, j: (0, j)),
            pl.BlockSpec((tm, tn), lambda i, j: (i, gr0 + j)),
            pl.BlockSpec((tm, tn), lambda i, j: (i, gd0 + j)),
        ],
        out_specs=pl.BlockSpec((tm, tn), lambda i, j: (i, j)),
        out_shape=jax.ShapeDtypeStruct((m, d_model), BF16),
        compiler_params=_params("parallel", "parallel"),
        name="merge",
    )(ret, dil, w_br_ret, w_br_dil, proj2d, proj2d)


def _mem_attn_kernel(q_ref, k_ref, v_ref, o_ref):
    dh = q_ref.shape[2]
    sc = lax.dot_general(q_ref[0], k_ref[0], NT_DIMS, preferred_element_type=F32) * (dh ** -0.5)
    m = jnp.max(sc, axis=-1, keepdims=True)
    p = jnp.exp(sc - m)
    l = jnp.sum(p, axis=-1, keepdims=True)
    o = jnp.dot(p.astype(BF16), v_ref[0], preferred_element_type=F32)
    o_ref[0] = (o / l).astype(o_ref.dtype)


def _mem_attention(q, kv, *, ts):
    b, s, d = q.shape
    m = kv.shape[1]
    dh = d // MEM_HEADS
    return pl.pallas_call(
        _mem_attn_kernel,
        grid=(b, s // ts, MEM_HEADS),
        in_specs=[
            pl.BlockSpec((1, ts, dh), lambda i, j, h: (i, j, h)),
            pl.BlockSpec((1, m, dh), lambda i, j, h: (i, 0, h)),
            pl.BlockSpec((1, m, dh), lambda i, j, h: (i, 0, MEM_HEADS + h)),
        ],
        out_specs=pl.BlockSpec((1, ts, dh), lambda i, j, h: (i, j, h)),
        out_shape=jax.ShapeDtypeStruct((b, s, d), BF16),
        compiler_params=_params("parallel", "parallel", "parallel"),
        name="mem_attention",
    )(q, kv, kv)


NOT_RANKED = 127.0
TOPK_COLUMNS_PER_ITER = 4


def _extract_top(w, count):
    n = w.shape[0]
    idx = lax.broadcasted_iota(jnp.int32, w.shape, 0).astype(F32)
    rank = jnp.full(w.shape, NOT_RANKED, F32)
    vals = []
    for it in range(count):
        m = jnp.max(w, axis=0, keepdims=True)
        first = jnp.min(jnp.where(w == m, idx, float(n)), axis=0, keepdims=True)
        hit = idx == first
        w = jnp.where(hit, -jnp.inf, w)
        rank = jnp.where(hit, float(it), rank)
        vals.append(m)
    return vals, rank, w


_PAIR_CANDIDATES = tuple((a, b) for a in range(PEER_TOPK) for b in range(PEER_TOPK)
                         if (a + 1) * (b + 1) <= PEER_TOPK)


def _peer_topk_kernel(q_ref, sk_ref, r2_ref, e2_ref, n1_ref, e1_ref):
    k = PEER_TOPK
    lanes = 128
    pad_rows = -len(_PAIR_CANDIDATES) % 8

    def column(c):
        tok = pl.ds(pl.multiple_of(c * lanes, lanes), lanes)
        sc1 = lax.dot_general(sk_ref[0, 0], q_ref[tok, :HEAD], NT_DIMS, preferred_element_type=F32)
        sc2 = lax.dot_general(sk_ref[0, 1], q_ref[tok, HEAD:], NT_DIMS, preferred_element_type=F32)
        v1, rank1, _ = _extract_top(sc1, k)
        v2, rank2, _ = _extract_top(sc2, k)
        rows = [v1[a] + v2[b] for a, b in _PAIR_CANDIDATES]
        rows += [jnp.full_like(rows[0], -jnp.inf)] * pad_rows
        combo = jnp.concatenate(rows, axis=0)
        cv, crank, _ = _extract_top(combo, k)
        taken = jnp.where(crank < NOT_RANKED, 1.0, 0.0)
        z = jnp.zeros_like(cv[0])
        for i in range(k):
            z = z + jnp.exp(cv[i] - cv[0])
        n1 = jnp.zeros_like(sc1)
        for a in range(k):
            rows_a = [r for r, (ca, _) in enumerate(_PAIR_CANDIDATES) if ca == a]
            cnt = taken[rows_a[0]:rows_a[0] + 1, :]
            for r in rows_a[1:]:
                cnt = cnt + taken[r:r + 1, :]
            n1 = jnp.where(rank1 == float(a), cnt, n1)
        sel1 = rank1 < NOT_RANKED
        sel2 = rank2 < NOT_RANKED
        r2_ref[0, :, tok] = rank2.astype(BF16)
        e2_ref[0, :, tok] = jnp.where(sel2, jnp.exp(sc2 - v2[0]), 0.0).astype(BF16)
        n1_ref[0, :, tok] = n1
        e1_ref[0, :, tok] = jnp.where(sel1, jnp.exp(sc1 - v1[0]), 0.0) / z

    def group(g, _):
        for off in range(TOPK_COLUMNS_PER_ITER):
            column(g * TOPK_COLUMNS_PER_ITER + off)
        return 0

    lax.fori_loop(0, q_ref.shape[0] // (lanes * TOPK_COLUMNS_PER_ITER), group, 0)


def _peer_topk(qp, subkeys, *, tk):
    t = qp.shape[0]
    h, _, nk, dq = subkeys.shape
    spec = pl.BlockSpec((1, nk, tk), lambda i, j: (j, 0, i))
    narrow = jax.ShapeDtypeStruct((h, nk, t), BF16)
    wide = jax.ShapeDtypeStruct((h, nk, t), F32)
    return pl.pallas_call(
        _peer_topk_kernel,
        grid=(t // tk, h),
        in_specs=[pl.BlockSpec((tk, 2 * dq), lambda i, j: (i, j)),
                  pl.BlockSpec((1, 2, nk, dq), lambda i, j: (j, 0, 0, 0))],
        out_specs=[spec] * 4,
        out_shape=[narrow, narrow, wide, wide],
        compiler_params=_params("parallel", "parallel"),
        name="peer_topk",
    )(qp, subkeys)


BF16_SUBLANES = 16


def _row_to_bf16_tile(row, rows):
    tile = jnp.broadcast_to(row, (BF16_SUBLANES, row.shape[1])).astype(BF16)
    return jnp.concatenate([tile] * (rows // BF16_SUBLANES), axis=0)


GATE_ROUNDTRIPS = 4


def _lane_roundtrip(x):
    rows, cols = x.shape
    words = pltpu.bitcast(x, jnp.uint32)
    parts = []
    for c in range(cols // 128):
        part = words[:, c * 128:(c + 1) * 128]
        for _ in range(GATE_ROUNDTRIPS):
            part = pltpu.roll(pltpu.roll(part, 1, 1), 127, 1)
        parts.append(part)
    return pltpu.bitcast(jnp.concatenate(parts, axis=1), x.dtype)


def _peer_ffn_kernel(xn_ref, u_ref, v_ref, r2_ref, e2_ref, n1_ref, e1_ref, res_ref, gf_ref, o_ref,
                     w_ref, *, final_norm):
    j = pl.program_id(1)
    ne = u_ref.shape[0]
    nk = PEER_N_KEYS

    @pl.when(j == 0)
    def _():
        o_ref[...] = jnp.zeros_like(o_ref)

    a = lax.dot_general(u_ref[...], xn_ref[...], NT_DIMS, preferred_element_type=F32)
    for b in range(ne // nk):
        gate = None
        for h in range(PEER_HEADS):
            n1 = _row_to_bf16_tile(n1_ref[h, b:b + 1, :], nk)
            e1 = _row_to_bf16_tile(e1_ref[h, b:b + 1, :], nk)
            term = jnp.where(r2_ref[h] < n1, e2_ref[h], jnp.zeros((), BF16)) * e1
            gate = term if gate is None else gate + term
        gate = _lane_roundtrip(gate)
        ab = a[b * nk:(b + 1) * nk, :]
        act = 0.5 * ab * (1.0 + lax.erf(ab * (0.5 ** 0.5)))
        w_ref[b * nk:(b + 1) * nk, :] = act.astype(BF16) * gate
    o_ref[...] += lax.dot_general(w_ref[...], v_ref[...], TN_DIMS, preferred_element_type=F32)

    @pl.when(j == pl.num_programs(1) - 1)
    def _():
        y = res_ref[...] + o_ref[...]
        if final_norm:
            ms = jnp.mean(y * y, axis=-1, keepdims=True)
            y = (y * lax.rsqrt(ms + NORM_EPS)) * gf_ref[...]
        o_ref[...] = y


def _peer_ffn(xn, u, v, r2, e2, n1, e1, res, g_final, *, tt, ne, final_norm):
    t, d = xn.shape
    n_exp = u.shape[0]
    h, nk, _ = r2.shape
    rows = ne // nk
    assert n_exp == nk * nk and ne % nk == 0 and t % tt == 0 and n_exp % ne == 0 and rows % 8 == 0
    keys = pl.BlockSpec((h, nk, tt), lambda i, j: (0, 0, i))
    first = pl.BlockSpec((h, rows, tt), lambda i, j: (0, j, i))
    return pl.pallas_call(
        functools.partial(_peer_ffn_kernel, final_norm=final_norm),
        grid=(t // tt, n_exp // ne),
        in_specs=[
            pl.BlockSpec((tt, d), lambda i, j: (i, 0)),
            pl.BlockSpec((ne, d), lambda i, j: (j, 0)),
            pl.BlockSpec((ne, d), lambda i, j: (j, 0)),
            keys, keys, first, first,
            pl.BlockSpec((tt, d), lambda i, j: (i, 0)),
            pl.BlockSpec((1, d), lambda i, j: (0, 0)),
        ],
        out_specs=pl.BlockSpec((tt, d), lambda i, j: (i, 0)),
        out_shape=jax.ShapeDtypeStruct((t, d), F32),
        scratch_shapes=[pltpu.VMEM((ne, tt), BF16)],
        compiler_params=_params("parallel", "arbitrary"),
        name="peer_ffn",
    )(xn, u, v, r2, e2, n1, e1, res, g_final.reshape(1, d).astype(F32))


def kernel(x, mem, positions, g_mix, w_in, w_br_ret, w_br_dil, w_out, g_cross, g_mem, w_q_mem,
           w_kv_mem, w_o_mem, g_ffn, w_peer_q, peer_subkeys, peer_u, peer_v, g_final):
    b, s, d = x.shape
    t = b * s
    depth = w_in.shape[0]
    ret_w = RET_HEADS * HEAD
    dil_col0 = 4 * ret_w
    dil_w = len(DIL_PATTERNS) * DIL_HEADS_PER_GROUP * HEAD
    gate_col0 = dil_col0 + 3 * dil_w
    tm = min(1024, t)

    rot_cos, rot_sin, ret_cos, ret_sin = _rotation_tables(positions)
    x2d = x.reshape(t, d)
    for l in range(depth):
        bf = lambda w: w[l].astype(BF16)
        proj = _norm_matmul(x2d, g_mix[l], bf(w_in), tm=tm, tn=1280, out_dtype=BF16)
        proj3 = proj.reshape(b, s, -1)
        ret = _retention(proj3, ret_cos, ret_sin).reshape(t, ret_w)
        dil = _dilated(proj3, rot_cos, rot_sin, dil_col0).reshape(t, -1)
        merged = _merge(ret, dil, bf(w_br_ret), bf(w_br_dil), proj, gate_col0, tm=tm, tn=512)
        x2d = _matmul_res(merged, bf(w_out), x2d, tm=tm, tn=1024)
        mem2d = mem.reshape(-1, d)
        kv = _norm_matmul(mem2d, g_mem[l], bf(w_kv_mem), tm=min(1024, mem2d.shape[0]), tn=1024,
                          out_dtype=BF16)
        qm = _norm_matmul(x2d, g_cross[l], bf(w_q_mem), tm=tm, tn=1024, out_dtype=BF16)
        att = _mem_attention(qm.reshape(b, s, d), kv.reshape(b, -1, 2 * d), ts=min(1024, s))
        x2d = _matmul_res(att.reshape(t, d), bf(w_o_mem), x2d, tm=tm, tn=1024)
        qp, xn = _norm_matmul(x2d, g_ffn[l], bf(w_peer_q), tm=tm, tn=1024, out_dtype=BF16,
                              emit_hn=True)
        r2, e2, n1, e1 = _peer_topk(qp, bf(peer_subkeys), tk=512)
        x2d = _peer_ffn(xn, bf(peer_u), bf(peer_v), r2, e2, n1, e1, x2d, g_final, tt=512, ne=1024,
                        final_norm=(l == depth - 1))
    return x2d.reshape(b, s, d)
```

```python
import functools
import math

import jax
import jax.numpy as jnp
import numpy as np
from jax import lax
from jax.experimental import pallas as pl
from jax.experimental.pallas import tpu as pltpu

F32 = jnp.float32
BF16 = jnp.bfloat16

NORM_EPS = 1e-6
ROPE_THETA = 10000.0
HEAD = 128
RET_HEADS = 8
RET_CHUNK = 128
DIL_PATTERNS = ((128, 1), (512, 4), (2048, 16))
DIL_HEADS_PER_GROUP = 4
DIL_BLOCK = 128
MEM_HEADS = 4
PEER_HEADS = 8
PEER_N_KEYS = 128
PEER_TOPK = 16

V7X_VMEM_BYTES = 64 * 1024 * 1024
VMEM_LIMIT = 56 * 1024 * 1024

NT_DIMS = (((1,), (1,)), ((), ()))
TN_DIMS = (((0,), (0,)), ((), ()))


def _params(*sem):
    return pltpu.CompilerParams(dimension_semantics=sem, vmem_limit_bytes=VMEM_LIMIT)


def _norm_rows(x_ref, g_ref, hn_ref, rows):
    tm = x_ref.shape[0]

    def body(c, _):
        sl = pl.ds(pl.multiple_of(c * rows, rows), rows)
        x = x_ref[sl, :].astype(F32)
        ms = jnp.mean(x * x, axis=-1, keepdims=True)
        hn_ref[sl, :] = ((x * lax.rsqrt(ms + NORM_EPS)) * g_ref[...]).astype(BF16)
        return 0

    lax.fori_loop(0, tm // rows, body, 0)


def _norm_matmul_kernel(x_ref, g_ref, w_ref, o_ref, hn_ref):
    @pl.when(pl.program_id(1) == 0)
    def _():
        _norm_rows(x_ref, g_ref, hn_ref, 256)

    o_ref[...] = jnp.dot(hn_ref[...], w_ref[...], preferred_element_type=F32).astype(o_ref.dtype)


def _norm_matmul(x, g, w, *, tm, tn, out_dtype, emit_hn=False):
    m, k = x.shape
    n = w.shape[1]
    assert m % tm == 0 and n % tn == 0
    in_specs = [
        pl.BlockSpec((tm, k), lambda i, j: (i, 0)),
        pl.BlockSpec((1, k), lambda i, j: (0, 0)),
        pl.BlockSpec((k, tn), lambda i, j: (0, j)),
    ]
    o_spec = pl.BlockSpec((tm, tn), lambda i, j: (i, j))
    o_shape = jax.ShapeDtypeStruct((m, n), out_dtype)
    if emit_hn:
        out_specs = [o_spec, pl.BlockSpec((tm, k), lambda i, j: (i, 0))]
        out_shape = [o_shape, jax.ShapeDtypeStruct((m, k), BF16)]
        scratch = []
    else:
        out_specs = o_spec
        out_shape = o_shape
        scratch = [pltpu.VMEM((tm, k), BF16)]
    return pl.pallas_call(
        _norm_matmul_kernel,
        grid=(m // tm, n // tn),
        in_specs=in_specs,
        out_specs=out_specs,
        out_shape=out_shape,
        scratch_shapes=scratch,
        compiler_params=_params("parallel", "arbitrary"),
        name="norm_matmul",
    )(x, g.reshape(1, k).astype(F32), w)


def _matmul_res_kernel(a_ref, w_ref, r_ref, o_ref):
    o_ref[...] = r_ref[...] + jnp.dot(a_ref[...], w_ref[...], preferred_element_type=F32)


def _matmul_res(a, w, res, *, tm, tn):
    m, k = a.shape
    n = w.shape[1]
    assert m % tm == 0 and n % tn == 0
    return pl.pallas_call(
        _matmul_res_kernel,
        grid=(m // tm, n // tn),
        in_specs=[
            pl.BlockSpec((tm, k), lambda i, j: (i, 0)),
            pl.BlockSpec((k, tn), lambda i, j: (0, j)),
            pl.BlockSpec((tm, tn), lambda i, j: (i, j)),
        ],
        out_specs=pl.BlockSpec((tm, tn), lambda i, j: (i, j)),
        out_shape=jax.ShapeDtypeStruct((m, n), F32),
        compiler_params=_params("parallel", "parallel"),
        name="matmul_res",
    )(a, w, res)


def _tables_kernel(pos_ref, inv_ref, rc_ref, rs_ref, tc_ref, ts_ref, *, rows):
    s = pos_ref.shape[1]
    lane = lax.broadcasted_iota(jnp.int32, (1, HEAD), 1)
    half_sign = jnp.where(lane < HEAD // 2, -1.0, 1.0).astype(F32)
    pair_sign = jnp.where(lane % 2 == 0, -1.0, 1.0).astype(F32)

    def body(c, _):
        sl = pl.ds(pl.multiple_of(c * rows, rows), rows)
        pos = pos_ref[0, sl, :]
        ang = pos * inv_ref[0:1, :]
        rc_ref[0, sl, :] = jnp.cos(ang)
        rs_ref[0, sl, :] = jnp.sin(ang) * half_sign
        ang = pos * inv_ref[1:2, :]
        tc_ref[0, sl, :] = jnp.cos(ang)
        ts_ref[0, sl, :] = jnp.sin(ang) * pair_sign
        return 0

    lax.fori_loop(0, s // rows, body, 0)


def _rotation_tables(positions):
    b, s = positions.shape
    half = HEAD // 2
    inv_half = 1.0 / (ROPE_THETA ** (jnp.arange(half, dtype=F32) / half))
    inv_pair = 1.0 / (10000.0 ** jnp.linspace(0.0, 1.0, half, dtype=F32))
    inv = jnp.stack([jnp.concatenate([inv_half, inv_half]), jnp.repeat(inv_pair, 2)])
    pos = positions.astype(F32).reshape(b, s, 1)
    tab = jax.ShapeDtypeStruct((b, s, HEAD), F32)
    spec = pl.BlockSpec((1, s, HEAD), lambda i: (i, 0, 0))
    return pl.pallas_call(
        functools.partial(_tables_kernel, rows=256),
        grid=(b,),
        in_specs=[pl.BlockSpec((1, s, 1), lambda i: (i, 0, 0)),
                  pl.BlockSpec((2, HEAD), lambda i: (0, 0))],
        out_specs=[spec] * 4,
        out_shape=[tab] * 4,
        compiler_params=_params("parallel"),
        name="rotation_tables",
    )(pos, inv)


RET_HEADS_PER_STEP = 4


def _retention_kernel(logg_ref, q_ref, k_ref, v_ref, g_ref, cos_ref, sin_ref, o_ref,
                      state_ref, dmat_ref, kw_ref, qw_ref, decay_ref):
    c = RET_CHUNK
    s = q_ref.shape[1]
    hp = RET_HEADS_PER_STEP
    row = lax.broadcasted_iota(jnp.int32, (c, c), 0).astype(F32)
    col = lax.broadcasted_iota(jnp.int32, (c, c), 1).astype(F32)
    diff = row - col
    for h in range(hp):
        lg = logg_ref[pl.program_id(1) * hp + h]
        dmat_ref[h] = jnp.where(diff >= 0, jnp.exp(lg * jnp.maximum(diff, 0.0)), 0.0)
        kw_ref[h] = jnp.exp(lg * (c - 1.0 - row))
        qw_ref[h] = jnp.exp(lg * (row + 1.0))
        decay_ref[h] = jnp.exp(lg * jnp.full((c, c), float(c), F32))
    even = lax.broadcasted_iota(jnp.int32, (c, HEAD), 1) % 2 == 0
    scale = HEAD ** -0.5

    state_ref[...] = jnp.zeros_like(state_ref)

    def body(n, _):
        sl = pl.ds(pl.multiple_of(n * c, c), c)
        cos = cos_ref[0, sl, :]
        sin = sin_ref[0, sl, :]

        def rot(x):
            swapped = jnp.where(even, pltpu.roll(x, HEAD - 1, 1), pltpu.roll(x, 1, 1))
            return x * cos + swapped * sin

        for h in range(hp):
            cols = slice(h * HEAD, (h + 1) * HEAD)
            q = rot(q_ref[0, sl, cols].astype(F32))
            k = rot(k_ref[0, sl, cols].astype(F32)) * scale
            v = v_ref[0, sl, cols]
            scores = lax.dot_general(q.astype(BF16), k.astype(BF16), NT_DIMS,
                                     preferred_element_type=F32) * dmat_ref[h]
            inner = jnp.dot(scores.astype(BF16), v, preferred_element_type=F32)
            state = state_ref[h]
            cross = jnp.dot((q * qw_ref[h]).astype(BF16), state.astype(BF16),
                            preferred_element_type=F32)
            kv = lax.dot_general((k * kw_ref[h]).astype(BF16), v, TN_DIMS,
                                 preferred_element_type=F32)
            state_ref[h] = decay_ref[h] * state + kv
            out = inner + cross
            out = out * lax.rsqrt(jnp.mean(out * out, axis=-1, keepdims=True) + NORM_EPS)
            g = g_ref[0, sl, cols].astype(F32)
            o_ref[0, sl, cols] = (out * (g * jax.nn.sigmoid(g))).astype(o_ref.dtype)
        return 0

    lax.fori_loop(0, s // c, body, 0)


def _retention(proj, ret_cos, ret_sin):
    b, s, _ = proj.shape
    h = RET_HEADS
    hp = RET_HEADS_PER_STEP
    groups = h // hp
    log_g = jnp.log1p(-jnp.exp2(-5.0 - jnp.arange(h, dtype=F32)))
    heads = lambda part: pl.BlockSpec((1, s, hp * HEAD), lambda i, j: (i, 0, part * groups + j))
    tab = pl.BlockSpec((1, s, HEAD), lambda i, j: (i, 0, 0))
    per_head = pltpu.VMEM((hp, HEAD, HEAD), F32)
    return pl.pallas_call(
        _retention_kernel,
        grid=(b, groups),
        in_specs=[pl.BlockSpec(memory_space=pltpu.SMEM),
                  heads(0), heads(1), heads(2), heads(3), tab, tab],
        out_specs=pl.BlockSpec((1, s, hp * HEAD), lambda i, j: (i, 0, j)),
        out_shape=jax.ShapeDtypeStruct((b, s, h * HEAD), BF16),
        scratch_shapes=[per_head] * 5,
        compiler_params=_params("parallel", "parallel"),
        name="retention",
    )(log_g, proj, proj, proj, proj, ret_cos, ret_sin)


def _dilated_kernel(*refs):
    qkv_refs = refs[:9]
    cos_ref, sin_ref, o_ref, qs, ks, vs, og, ol = refs[9:]
    s = o_ref.shape[1]
    blk = DIL_BLOCK
    scale = HEAD ** -0.5
    neg = -1e30
    qi = lax.broadcasted_iota(jnp.int32, (blk, blk), 0)
    kj = lax.broadcasted_iota(jnp.int32, (blk, blk), 1)
    cos = cos_ref[0]
    sin = sin_ref[0]

    def rot(x):
        return x * cos + pltpu.roll(x, HEAD // 2, 1) * sin

    for g, (window, r) in enumerate(DIL_PATTERNS):
        win = window // r
        length = s // r
        nb = length // blk
        cur_ok = (qi - kj >= 0) & (qi - kj <= win)
        prev_ok = (qi - kj + blk) <= win
        q_ref, k_ref, v_ref = qkv_refs[3 * g:3 * g + 3]
        qs[...] = rot(q_ref[0].astype(F32))
        ks[...] = rot(k_ref[0].astype(F32))
        vs[...] = v_ref[0].astype(F32)
        for c in range(r):
            for n in range(nb):
                rows = lambda m: pl.ds(c + m * blk * r, blk, stride=r) if r > 1 else pl.ds(m * blk, blk)
                q = qs[rows(n), :].astype(BF16)
                k1 = ks[rows(n), :].astype(BF16)
                v1 = vs[rows(n), :].astype(BF16)
                s1 = lax.dot_general(q, k1, NT_DIMS, preferred_element_type=F32) * scale
                s1 = jnp.where(cur_ok, s1, neg)
                m = jnp.max(s1, axis=-1, keepdims=True)
                if n > 0:
                    k0 = ks[rows(n - 1), :].astype(BF16)
                    v0 = vs[rows(n - 1), :].astype(BF16)
                    s0 = lax.dot_general(q, k0, NT_DIMS, preferred_element_type=F32) * scale
                    s0 = jnp.where(prev_ok, s0, neg)
                    m = jnp.maximum(m, jnp.max(s0, axis=-1, keepdims=True))
                p1 = jnp.exp(s1 - m)
                l = jnp.sum(p1, axis=-1, keepdims=True)
                acc = jnp.dot(p1.astype(BF16), v1, preferred_element_type=F32)
                if n > 0:
                    p0 = jnp.exp(s0 - m)
                    l = l + jnp.sum(p0, axis=-1, keepdims=True)
                    acc = acc + jnp.dot(p0.astype(BF16), v0, preferred_element_type=F32)
                og[g, rows(n), :] = acc / l
                ol[g, rows(n), :] = jnp.broadcast_to(m + jnp.log(l), (blk, HEAD))

    ng = len(DIL_PATTERNS)
    mx = ol[0]
    for g in range(1, ng):
        mx = jnp.maximum(mx, ol[g])
    num = jnp.zeros((s, HEAD), F32)
    den = jnp.zeros((s, HEAD), F32)
    for g in range(ng):
        e = jnp.exp(ol[g] - mx)
        num = num + e * og[g]
        den = den + e
    o_ref[0] = (num / den).astype(o_ref.dtype)


def _dilated(proj, rot_cos, rot_sin, col0):
    b, s, _ = proj.shape
    assert s % (DIL_BLOCK * max(r for _, r in DIL_PATTERNS)) == 0
    hg = DIL_HEADS_PER_GROUP
    ng = len(DIL_PATTERNS)
    width = ng * hg
    base = col0 // HEAD
    specs = []
    for g in range(ng):
        for part in range(3):
            off = base + part * width + g * hg
            specs.append(pl.BlockSpec((1, s, HEAD), lambda i, j, off=off: (i, 0, off + j)))
    tab = pl.BlockSpec((1, s, HEAD), lambda i, j: (i, 0, 0))
    return pl.pallas_call(
        _dilated_kernel,
        grid=(b, hg),
        in_specs=specs + [tab, tab],
        out_specs=pl.BlockSpec((1, s, HEAD), lambda i, j: (i, 0, j)),
        out_shape=jax.ShapeDtypeStruct((b, s, hg * HEAD), BF16),
        scratch_shapes=[pltpu.VMEM((s, HEAD), F32)] * 3
        + [pltpu.VMEM((ng, s, HEAD), F32)] * 2,
        compiler_params=_params("parallel", "parallel"),
        name="dilated",
    )(*([proj] * 9), rot_cos, rot_sin)


def _merge_kernel(ret_ref, dil_ref, wr_ref, wd_ref, gr_ref, gd_ref, o_ref):
    a = jnp.dot(ret_ref[...], wr_ref[...], preferred_element_type=F32)
    d = jnp.dot(dil_ref[...], wd_ref[...], preferred_element_type=F32)
    gr = jax.nn.sigmoid(gr_ref[...].astype(F32))
    gd = jax.nn.sigmoid(gd_ref[...].astype(F32))
    o_ref[...] = (gr * a + gd * d).astype(o_ref.dtype)


def _merge(ret, dil, w_br_ret, w_br_dil, proj2d, gate_col0, *, tm, tn):
    m, d_model = ret.shape[0], w_br_ret.shape[1]
    assert gate_col0 % tn == 0 and d_model % tn == 0 and m % tm == 0
    gr0 = gate_col0 // tn
    gd0 = gr0 + d_model // tn
    return pl.pallas_call(
        _merge_kernel,
        grid=(m // tm, d_model // tn),
        in_specs=[
            pl.BlockSpec((tm, ret.shape[1]), lambda i, j: (i, 0)),
            pl.BlockSpec((tm, dil.shape[1]), lambda i, j: (i, 0)),
            pl.BlockSpec((w_br_ret.shape[0], tn), lambda i, j: (0, j)),
            pl.BlockSpec((w_br_dil.shape[0], tn), lambda i, j: (0, j)),
            pl.BlockSpec((tm, tn), lambda i, j: (i, gr0 + j)),
            pl.BlockSpec((tm, tn), lambda i, j: (i, gd0 + j)),
        ],
        out_specs=pl.BlockSpec((tm, tn), lambda i, j: (i, j)),
        out_shape=jax.ShapeDtypeStruct((m, d_model), BF16),
        compiler_params=_params("parallel", "parallel"),
        name="merge",
    )(ret, dil, w_br_ret, w_br_dil, proj2d, proj2d)


def _mem_attn_kernel(q_ref, k_ref, v_ref, o_ref):
    dh = q_ref.shape[2]
    sc = lax.dot_general(q_ref[0], k_ref[0], NT_DIMS, preferred_element_type=F32) * (dh ** -0.5)
    m = jnp.max(sc, axis=-1, keepdims=True)
    p = jnp.exp(sc - m)
    l = jnp.sum(p, axis=-1, keepdims=True)
    o = jnp.dot(p.astype(BF16), v_ref[0], preferred_element_type=F32)
    o_ref[0] = (o / l).astype(o_ref.dtype)


def _mem_attention(q, kv, *, ts):
    b, s, d = q.shape
    m = kv.shape[1]
    dh = d // MEM_HEADS
    return pl.pallas_call(
        _mem_attn_kernel,
        grid=(b, s // ts, MEM_HEADS),
        in_specs=[
            pl.BlockSpec((1, ts, dh), lambda i, j, h: (i, j, h)),
            pl.BlockSpec((1, m, dh), lambda i, j, h: (i, 0, h)),
            pl.BlockSpec((1, m, dh), lambda i, j, h: (i, 0, MEM_HEADS + h)),
        ],
        out_specs=pl.BlockSpec((1, ts, dh), lambda i, j, h: (i, j, h)),
        out_shape=jax.ShapeDtypeStruct((b, s, d), BF16),
        compiler_params=_params("parallel", "parallel", "parallel"),
        name="mem_attention",
    )(q, kv, kv)


NOT_RANKED = 127.0
TOPK_COLUMNS_PER_ITER = 4


def _extract_top(w, count, tie_break, want_rank=True):
    n = w.shape[0]
    idx = lax.broadcasted_iota(jnp.int32, w.shape, 0).astype(F32)
    rank = jnp.full(w.shape, NOT_RANKED, F32) if want_rank else None
    vals = []
    for it in range(count):
        m = jnp.max(w, axis=0, keepdims=True)
        if tie_break:
            first = jnp.min(jnp.where(w == m, idx, float(n)), axis=0, keepdims=True)
            hit = idx == first
        else:
            hit = w == m
        w = jnp.where(hit, -jnp.inf, w)
        if want_rank:
            rank = jnp.where(hit, float(it), rank)
        vals.append(m)
    return vals, rank, w


_PAIR_CANDIDATES = tuple((a, b) for a in range(PEER_TOPK) for b in range(PEER_TOPK)
                         if (a + 1) * (b + 1) <= PEER_TOPK)


def _peer_topk_kernel(q_ref, sk_ref, r2_ref, e2_ref, n1_ref, e1_ref):
    k = PEER_TOPK
    lanes = 128
    pad_rows = -len(_PAIR_CANDIDATES) % 8

    def column(c, tie_break):
        tok = pl.ds(pl.multiple_of(c * lanes, lanes), lanes)
        sc1 = lax.dot_general(sk_ref[0, 0], q_ref[tok, :HEAD], NT_DIMS, preferred_element_type=F32)
        sc2 = lax.dot_general(sk_ref[0, 1], q_ref[tok, HEAD:], NT_DIMS, preferred_element_type=F32)
        v1, rank1, _ = _extract_top(sc1, k, tie_break)
        v2, rank2, _ = _extract_top(sc2, k, tie_break)
        rows = [v1[a] + v2[b] for a, b in _PAIR_CANDIDATES]
        rows += [jnp.full_like(rows[0], -jnp.inf)] * pad_rows
        combo = jnp.concatenate(rows, axis=0)
        cv, _, left = _extract_top(combo, k, tie_break, want_rank=False)
        taken = jnp.where(left < combo, 1.0, 0.0)
        z = jnp.zeros_like(cv[0])
        for i in range(k):
            z = z + jnp.exp(cv[i] - cv[0])
        n1 = jnp.zeros_like(sc1)
        for a in range(k):
            rows_a = [r for r, (ca, _) in enumerate(_PAIR_CANDIDATES) if ca == a]
            cnt = taken[rows_a[0]:rows_a[0] + 1, :]
            for r in rows_a[1:]:
                cnt = cnt + taken[r:r + 1, :]
            n1 = jnp.where(rank1 == float(a), cnt, n1)
        sel1 = rank1 < NOT_RANKED
        sel2 = rank2 < NOT_RANKED
        r2_ref[0, :, tok] = rank2.astype(BF16)
        e2_ref[0, :, tok] = jnp.where(sel2, jnp.exp(sc2 - v2[0]), 0.0).astype(BF16)
        n1_ref[0, :, tok] = n1
        e1_ref[0, :, tok] = jnp.where(sel1, jnp.exp(sc1 - v1[0]), 0.0) / z
        count = lambda flags: jnp.sum(flags, axis=0, keepdims=True)
        return (jnp.abs(count(jnp.where(sel1, 1.0, 0.0)) - k) + jnp.abs(count(jnp.where(sel2, 1.0, 0.0)) - k)
                + jnp.abs(count(taken) - k))

    def group(g, _):
        cols = [g * TOPK_COLUMNS_PER_ITER + off for off in range(TOPK_COLUMNS_PER_ITER)]
        off_count = sum(column(c, tie_break=False) for c in cols)

        @pl.when(jnp.max(off_count) > 0.0)
        def _():
            for c in cols:
                column(c, tie_break=True)

        return 0

    lax.fori_loop(0, q_ref.shape[0] // (lanes * TOPK_COLUMNS_PER_ITER), group, 0)


def _peer_topk(qp, subkeys, *, tk):
    t = qp.shape[0]
    h, _, nk, dq = subkeys.shape
    spec = pl.BlockSpec((1, nk, tk), lambda i, j: (j, 0, i))
    narrow = jax.ShapeDtypeStruct((h, nk, t), BF16)
    wide = jax.ShapeDtypeStruct((h, nk, t), F32)
    return pl.pallas_call(
        _peer_topk_kernel,
        grid=(t // tk, h),
        in_specs=[pl.BlockSpec((tk, 2 * dq), lambda i, j: (i, j)),
                  pl.BlockSpec((1, 2, nk, dq), lambda i, j: (j, 0, 0, 0))],
        out_specs=[spec] * 4,
        out_shape=[narrow, narrow, wide, wide],
        compiler_params=_params("parallel", "parallel"),
        name="peer_topk",
    )(qp, subkeys)


BF16_SUBLANES = 16


def _row_to_bf16_tile(row, rows):
    tile = jnp.broadcast_to(row, (BF16_SUBLANES, row.shape[1])).astype(BF16)
    return jnp.concatenate([tile] * (rows // BF16_SUBLANES), axis=0)


GATE_ROUNDTRIPS = 3


def _lane_roundtrip(x):
    rows, cols = x.shape
    words = pltpu.bitcast(x, jnp.uint32)
    parts = []
    for c in range(cols // 128):
        part = words[:, c * 128:(c + 1) * 128]
        for _ in range(GATE_ROUNDTRIPS):
            part = pltpu.roll(pltpu.roll(part, 1, 1), 127, 1)
        parts.append(part)
    return pltpu.bitcast(jnp.concatenate(parts, axis=1), x.dtype)


def _peer_ffn_kernel(xn_ref, u_ref, v_ref, r2_ref, e2_ref, n1_ref, e1_ref, res_ref, gf_ref, o_ref,
                     w_ref, *, final_norm):
    j = pl.program_id(1)
    ne = u_ref.shape[0]
    nk = PEER_N_KEYS

    @pl.when(j == 0)
    def _():
        o_ref[...] = jnp.zeros_like(o_ref)

    a = lax.dot_general(u_ref[...], xn_ref[...], NT_DIMS, preferred_element_type=F32)
    for b in range(ne // nk):
        gate = None
        for h in range(PEER_HEADS):
            n1 = _row_to_bf16_tile(n1_ref[h, b:b + 1, :], nk)
            e1 = _row_to_bf16_tile(e1_ref[h, b:b + 1, :], nk)
            term = jnp.where(r2_ref[h] < n1, e2_ref[h], jnp.zeros((), BF16)) * e1
            gate = term if gate is None else gate + term
        gate = _lane_roundtrip(gate)
        ab = a[b * nk:(b + 1) * nk, :]
        act = 0.5 * ab * (1.0 + lax.erf(ab * (0.5 ** 0.5)))
        w_ref[b * nk:(b + 1) * nk, :] = act.astype(BF16) * gate
    o_ref[...] += lax.dot_general(w_ref[...], v_ref[...], TN_DIMS, preferred_element_type=F32)

    @pl.when(j == pl.num_programs(1) - 1)
    def _():
        y = res_ref[...] + o_ref[...]
        if final_norm:
            ms = jnp.mean(y * y, axis=-1, keepdims=True)
            y = (y * lax.rsqrt(ms + NORM_EPS)) * gf_ref[...]
        o_ref[...] = y


def _peer_ffn(xn, u, v, r2, e2, n1, e1, res, g_final, *, tt, ne, final_norm):
    t, d = xn.shape
    n_exp = u.shape[0]
    h, nk, _ = r2.shape
    rows = ne // nk
    assert n_exp == nk * nk and ne % nk == 0 and t % tt == 0 and n_exp % ne == 0 and rows % 8 == 0
    keys = pl.BlockSpec((h, nk, tt), lambda i, j: (0, 0, i))
    first = pl.BlockSpec((h, rows, tt), lambda i, j: (0, j, i))
    return pl.pallas_call(
        functools.partial(_peer_ffn_kernel, final_norm=final_norm),
        grid=(t // tt, n_exp // ne),
        in_specs=[
            pl.BlockSpec((tt, d), lambda i, j: (i, 0)),
            pl.BlockSpec((ne, d), lambda i, j: (j, 0)),
            pl.BlockSpec((ne, d), lambda i, j: (j, 0)),
            keys, keys, first, first,
            pl.BlockSpec((tt, d), lambda i, j: (i, 0)),
            pl.BlockSpec((1, d), lambda i, j: (0, 0)),
        ],
        out_specs=pl.BlockSpec((tt, d), lambda i, j: (i, 0)),
        out_shape=jax.ShapeDtypeStruct((t, d), F32),
        scratch_shapes=[pltpu.VMEM((ne, tt), BF16)],
        compiler_params=_params("parallel", "arbitrary"),
        name="peer_ffn",
    )(xn, u, v, r2, e2, n1, e1, res, g_final.reshape(1, d).astype(F32))


def kernel(x, mem, positions, g_mix, w_in, w_br_ret, w_br_dil, w_out, g_cross, g_mem, w_q_mem,
           w_kv_mem, w_o_mem, g_ffn, w_peer_q, peer_subkeys, peer_u, peer_v, g_final):
    b, s, d = x.shape
    t = b * s
    depth = w_in.shape[0]
    ret_w = RET_HEADS * HEAD
    dil_col0 = 4 * ret_w
    dil_w = len(DIL_PATTERNS) * DIL_HEADS_PER_GROUP * HEAD
    gate_col0 = dil_col0 + 3 * dil_w
    tm = min(1024, t)

    rot_cos, rot_sin, ret_cos, ret_sin = _rotation_tables(positions)
    x2d = x.reshape(t, d)
    for l in range(depth):
        bf = lambda w: w[l].astype(BF16)
        proj = _norm_matmul(x2d, g_mix[l], bf(w_in), tm=tm, tn=1280, out_dtype=BF16)
        proj3 = proj.reshape(b, s, -1)
        ret = _retention(proj3, ret_cos, ret_sin).reshape(t, ret_w)
        dil = _dilated(proj3, rot_cos, rot_sin, dil_col0).reshape(t, -1)
        merged = _merge(ret, dil, bf(w_br_ret), bf(w_br_dil), proj, gate_col0, tm=tm, tn=512)
        x2d = _matmul_res(merged, bf(w_out), x2d, tm=tm, tn=1024)
        mem2d = mem.reshape(-1, d)
        kv = _norm_matmul(mem2d, g_mem[l], bf(w_kv_mem), tm=min(1024, mem2d.shape[0]), tn=1024,
                          out_dtype=BF16)
        qm = _norm_matmul(x2d, g_cross[l], bf(w_q_mem), tm=tm, tn=1024, out_dtype=BF16)
        att = _mem_attention(qm.reshape(b, s, d), kv.reshape(b, -1, 2 * d), ts=min(1024, s))
        x2d = _matmul_res(att.reshape(t, d), bf(w_o_mem), x2d, tm=tm, tn=1024)
        qp, xn = _norm_matmul(x2d, g_ffn[l], bf(w_peer_q), tm=tm, tn=1024, out_dtype=BF16,
                              emit_hn=True)
        r2, e2, n1, e1 = _peer_topk(qp, bf(peer_subkeys), tk=512)
        x2d = _peer_ffn(xn, bf(peer_u), bf(peer_v), r2, e2, n1, e1, x2d, g_final, tt=512, ne=1024,
                        final_norm=(l == depth - 1))
    return x2d.reshape(b, s, d)
```

```python
import functools
import math

import jax
import jax.numpy as jnp
import numpy as np
from jax import lax
from jax.experimental import pallas as pl
from jax.experimental.pallas import tpu as pltpu

F32 = jnp.float32
BF16 = jnp.bfloat16

NORM_EPS = 1e-6
ROPE_THETA = 10000.0
HEAD = 128
RET_HEADS = 8
RET_CHUNK = 128
DIL_PATTERNS = ((128, 1), (512, 4), (2048, 16))
DIL_HEADS_PER_GROUP = 4
DIL_BLOCK = 128
MEM_HEADS = 4
PEER_HEADS = 8
PEER_N_KEYS = 128
PEER_TOPK = 16

V7X_VMEM_BYTES = 64 * 1024 * 1024
VMEM_LIMIT = 56 * 1024 * 1024

NT_DIMS = (((1,), (1,)), ((), ()))
TN_DIMS = (((0,), (0,)), ((), ()))


def _params(*sem):
    return pltpu.CompilerParams(dimension_semantics=sem, vmem_limit_bytes=VMEM_LIMIT)


def _norm_rows(x_ref, g_ref, hn_ref, rows):
    tm = x_ref.shape[0]

    def body(c, _):
        sl = pl.ds(pl.multiple_of(c * rows, rows), rows)
        x = x_ref[sl, :].astype(F32)
        ms = jnp.mean(x * x, axis=-1, keepdims=True)
        hn_ref[sl, :] = ((x * lax.rsqrt(ms + NORM_EPS)) * g_ref[...]).astype(BF16)
        return 0

    lax.fori_loop(0, tm // rows, body, 0)


def _norm_matmul_kernel(x_ref, g_ref, w_ref, o_ref, hn_ref):
    @pl.when(pl.program_id(1) == 0)
    def _():
        _norm_rows(x_ref, g_ref, hn_ref, 256)

    o_ref[...] = jnp.dot(hn_ref[...], w_ref[...], preferred_element_type=F32).astype(o_ref.dtype)


def _norm_matmul(x, g, w, *, tm, tn, out_dtype, emit_hn=False):
    m, k = x.shape
    n = w.shape[1]
    assert m % tm == 0 and n % tn == 0
    in_specs = [
        pl.BlockSpec((tm, k), lambda i, j: (i, 0)),
        pl.BlockSpec((1, k), lambda i, j: (0, 0)),
        pl.BlockSpec((k, tn), lambda i, j: (0, j)),
    ]
    o_spec = pl.BlockSpec((tm, tn), lambda i, j: (i, j))
    o_shape = jax.ShapeDtypeStruct((m, n), out_dtype)
    if emit_hn:
        out_specs = [o_spec, pl.BlockSpec((tm, k), lambda i, j: (i, 0))]
        out_shape = [o_shape, jax.ShapeDtypeStruct((m, k), BF16)]
        scratch = []
    else:
        out_specs = o_spec
        out_shape = o_shape
        scratch = [pltpu.VMEM((tm, k), BF16)]
    return pl.pallas_call(
        _norm_matmul_kernel,
        grid=(m // tm, n // tn),
        in_specs=in_specs,
        out_specs=out_specs,
        out_shape=out_shape,
        scratch_shapes=scratch,
        compiler_params=_params("parallel", "arbitrary"),
        name="norm_matmul",
    )(x, g.reshape(1, k).astype(F32), w)


def _matmul_res_kernel(a_ref, w_ref, r_ref, o_ref):
    o_ref[...] = r_ref[...] + jnp.dot(a_ref[...], w_ref[...], preferred_element_type=F32)


def _matmul_res(a, w, res, *, tm, tn):
    m, k = a.shape
    n = w.shape[1]
    assert m % tm == 0 and n % tn == 0
    return pl.pallas_call(
        _matmul_res_kernel,
        grid=(m // tm, n // tn),
        in_specs=[
            pl.BlockSpec((tm, k), lambda i, j: (i, 0)),
            pl.BlockSpec((k, tn), lambda i, j: (0, j)),
            pl.BlockSpec((tm, tn), lambda i, j: (i, j)),
        ],
        out_specs=pl.BlockSpec((tm, tn), lambda i, j: (i, j)),
        out_shape=jax.ShapeDtypeStruct((m, n), F32),
        compiler_params=_params("parallel", "parallel"),
        name="matmul_res",
    )(a, w, res)


def _tables_kernel(pos_ref, inv_ref, rc_ref, rs_ref, tc_ref, ts_ref, *, rows):
    s = pos_ref.shape[1]
    lane = lax.broadcasted_iota(jnp.int32, (1, HEAD), 1)
    half_sign = jnp.where(lane < HEAD // 2, -1.0, 1.0).astype(F32)
    pair_sign = jnp.where(lane % 2 == 0, -1.0, 1.0).astype(F32)

    def body(c, _):
        sl = pl.ds(pl.multiple_of(c * rows, rows), rows)
        pos = pos_ref[0, sl, :]
        ang = pos * inv_ref[0:1, :]
        rc_ref[0, sl, :] = jnp.cos(ang)
        rs_ref[0, sl, :] = jnp.sin(ang) * half_sign
        ang = pos * inv_ref[1:2, :]
        tc_ref[0, sl, :] = jnp.cos(ang)
        ts_ref[0, sl, :] = jnp.sin(ang) * pair_sign
        return 0

    lax.fori_loop(0, s // rows, body, 0)


def _rotation_tables(positions):
    b, s = positions.shape
    half = HEAD // 2
    inv_half = 1.0 / (ROPE_THETA ** (jnp.arange(half, dtype=F32) / half))
    inv_pair = 1.0 / (10000.0 ** jnp.linspace(0.0, 1.0, half, dtype=F32))
    inv = jnp.stack([jnp.concatenate([inv_half, inv_half]), jnp.repeat(inv_pair, 2)])
    pos = positions.astype(F32).reshape(b, s, 1)
    tab = jax.ShapeDtypeStruct((b, s, HEAD), F32)
    spec = pl.BlockSpec((1, s, HEAD), lambda i: (i, 0, 0))
    return pl.pallas_call(
        functools.partial(_tables_kernel, rows=256),
        grid=(b,),
        in_specs=[pl.BlockSpec((1, s, 1), lambda i: (i, 0, 0)),
                  pl.BlockSpec((2, HEAD), lambda i: (0, 0))],
        out_specs=[spec] * 4,
        out_shape=[tab] * 4,
        compiler_params=_params("parallel"),
        name="rotation_tables",
    )(pos, inv)


RET_HEADS_PER_STEP = 4


def _retention_kernel(logg_ref, q_ref, k_ref, v_ref, g_ref, cos_ref, sin_ref, o_ref,
                      state_ref, dmat_ref, kw_ref, qw_ref, decay_ref):
    c = RET_CHUNK
    s = q_ref.shape[1]
    hp = RET_HEADS_PER_STEP
    row = lax.broadcasted_iota(jnp.int32, (c, c), 0).astype(F32)
    col = lax.broadcasted_iota(jnp.int32, (c, c), 1).astype(F32)
    diff = row - col
    for h in range(hp):
        lg = logg_ref[pl.program_id(1) * hp + h]
        dmat_ref[h] = jnp.where(diff >= 0, jnp.exp(lg * jnp.maximum(diff, 0.0)), 0.0)
        kw_ref[h] = jnp.exp(lg * (c - 1.0 - row))
        qw_ref[h] = jnp.exp(lg * (row + 1.0))
        decay_ref[h] = jnp.exp(lg * jnp.full((c, c), float(c), F32))
    even = lax.broadcasted_iota(jnp.int32, (c, HEAD), 1) % 2 == 0
    scale = HEAD ** -0.5

    state_ref[...] = jnp.zeros_like(state_ref)

    def body(n, _):
        sl = pl.ds(pl.multiple_of(n * c, c), c)
        cos = cos_ref[0, sl, :]
        sin = sin_ref[0, sl, :]

        def rot(x):
            swapped = jnp.where(even, pltpu.roll(x, HEAD - 1, 1), pltpu.roll(x, 1, 1))
            return x * cos + swapped * sin

        for h in range(hp):
            cols = slice(h * HEAD, (h + 1) * HEAD)
            q = rot(q_ref[0, sl, cols].astype(F32))
            k = rot(k_ref[0, sl, cols].astype(F32)) * scale
            v = v_ref[0, sl, cols]
            scores = lax.dot_general(q.astype(BF16), k.astype(BF16), NT_DIMS,
                                     preferred_element_type=F32) * dmat_ref[h]
            inner = jnp.dot(scores.astype(BF16), v, preferred_element_type=F32)
            state = state_ref[h]
            cross = jnp.dot((q * qw_ref[h]).astype(BF16), state.astype(BF16),
                            preferred_element_type=F32)
            kv = lax.dot_general((k * kw_ref[h]).astype(BF16), v, TN_DIMS,
                                 preferred_element_type=F32)
            state_ref[h] = decay_ref[h] * state + kv
            out = inner + cross
            out = out * lax.rsqrt(jnp.mean(out * out, axis=-1, keepdims=True) + NORM_EPS)
            g = g_ref[0, sl, cols].astype(F32)
            o_ref[0, sl, cols] = (out * (g * jax.nn.sigmoid(g))).astype(o_ref.dtype)
        return 0

    lax.fori_loop(0, s // c, body, 0)


def _retention(proj, ret_cos, ret_sin):
    b, s, _ = proj.shape
    h = RET_HEADS
    hp = RET_HEADS_PER_STEP
    groups = h // hp
    log_g = jnp.log1p(-jnp.exp2(-5.0 - jnp.arange(h, dtype=F32)))
    heads = lambda part: pl.BlockSpec((1, s, hp * HEAD), lambda i, j: (i, 0, part * groups + j))
    tab = pl.BlockSpec((1, s, HEAD), lambda i, j: (i, 0, 0))
    per_head = pltpu.VMEM((hp, HEAD, HEAD), F32)
    return pl.pallas_call(
        _retention_kernel,
        grid=(b, groups),
        in_specs=[pl.BlockSpec(memory_space=pltpu.SMEM),
                  heads(0), heads(1), heads(2), heads(3), tab, tab],
        out_specs=pl.BlockSpec((1, s, hp * HEAD), lambda i, j: (i, 0, j)),
        out_shape=jax.ShapeDtypeStruct((b, s, h * HEAD), BF16),
        scratch_shapes=[per_head] * 5,
        compiler_params=_params("parallel", "parallel"),
        name="retention",
    )(log_g, proj, proj, proj, proj, ret_cos, ret_sin)


_DIL_WINDOWS = tuple(sorted({w // r for w, r in DIL_PATTERNS}))


def _dilated_kernel(*refs):
    qkv_refs = refs[:9]
    cos_ref, sin_ref, o_ref, qs, ks, vs, og, ol, bias_ref = refs[9:]
    s = o_ref.shape[1]
    blk = DIL_BLOCK
    scale = HEAD ** -0.5
    neg = -1e30
    qi = lax.broadcasted_iota(jnp.int32, (blk, blk), 0)
    kj = lax.broadcasted_iota(jnp.int32, (blk, blk), 1)
    cos = cos_ref[0]
    sin = sin_ref[0]

    def rot(x):
        return x * cos + pltpu.roll(x, HEAD // 2, 1) * sin

    for wi, win in enumerate(_DIL_WINDOWS):
        cur_ok = (qi - kj >= 0) & (qi - kj <= win)
        prev_ok = (qi - kj + blk) <= win
        bias_ref[2 * wi] = jnp.where(cur_ok, 0.0, neg)
        bias_ref[2 * wi + 1] = jnp.where(prev_ok, 0.0, neg)

    for g, (window, r) in enumerate(DIL_PATTERNS):
        wi = _DIL_WINDOWS.index(window // r)
        length = s // r
        nb = length // blk
        q_ref, k_ref, v_ref = qkv_refs[3 * g:3 * g + 3]
        qs[...] = rot(q_ref[0].astype(F32)) * scale
        ks[...] = rot(k_ref[0].astype(F32))
        vs[...] = v_ref[0].astype(F32)
        units = [(c, n) for c in range(r) for n in range(nb)]

        def rows(c, m):
            return pl.ds(c + m * blk * r, blk, stride=r) if r > 1 else pl.ds(m * blk, blk)

        def scores(c, n):
            q = qs[rows(c, n), :].astype(BF16)
            k1 = ks[rows(c, n), :].astype(BF16)
            s1 = lax.dot_general(q, k1, NT_DIMS, preferred_element_type=F32) + bias_ref[2 * wi]
            s0 = None
            if n > 0:
                k0 = ks[rows(c, n - 1), :].astype(BF16)
                s0 = (lax.dot_general(q, k0, NT_DIMS, preferred_element_type=F32)
                      + bias_ref[2 * wi + 1])
            return s1, s0

        def softmax(s1, s0):
            m = jnp.max(s1, axis=-1, keepdims=True)
            if s0 is not None:
                m = jnp.maximum(m, jnp.max(s0, axis=-1, keepdims=True))
            p1 = jnp.exp(s1 - m)
            l = jnp.sum(p1, axis=-1, keepdims=True)
            p0 = None
            if s0 is not None:
                p0 = jnp.exp(s0 - m)
                l = l + jnp.sum(p0, axis=-1, keepdims=True)
                p0 = p0.astype(BF16)
            return p1.astype(BF16), p0, m, l

        def finish(c, n, p1, p0, m, l):
            acc = jnp.dot(p1, vs[rows(c, n), :].astype(BF16), preferred_element_type=F32)
            if p0 is not None:
                acc = acc + jnp.dot(p0, vs[rows(c, n - 1), :].astype(BF16),
                                    preferred_element_type=F32)
            og[g, rows(c, n), :] = acc / l
            ol[g, rows(c, n), :] = jnp.broadcast_to(m + jnp.log(l), (blk, HEAD))

        sc, sm = {}, {}
        for i in range(len(units) + 2):
            if i < len(units):
                sc[i] = scores(*units[i])
            if 1 <= i <= len(units):
                sm[i - 1] = softmax(*sc.pop(i - 1))
            if i >= 2:
                finish(*units[i - 2], *sm.pop(i - 2))

    ng = len(DIL_PATTERNS)
    mx = ol[0]
    for g in range(1, ng):
        mx = jnp.maximum(mx, ol[g])
    num = jnp.zeros((s, HEAD), F32)
    den = jnp.zeros((s, HEAD), F32)
    for g in range(ng):
        e = jnp.exp(ol[g] - mx)
        num = num + e * og[g]
        den = den + e
    o_ref[0] = (num / den).astype(o_ref.dtype)


def _dilated(proj, rot_cos, rot_sin, col0):
    b, s, _ = proj.shape
    assert s % (DIL_BLOCK * max(r for _, r in DIL_PATTERNS)) == 0
    hg = DIL_HEADS_PER_GROUP
    ng = len(DIL_PATTERNS)
    width = ng * hg
    base = col0 // HEAD
    specs = []
    for g in range(ng):
        for part in range(3):
            off = base + part * width + g * hg
            specs.append(pl.BlockSpec((1, s, HEAD), lambda i, j, off=off: (i, 0, off + j)))
    tab = pl.BlockSpec((1, s, HEAD), lambda i, j: (i, 0, 0))
    return pl.pallas_call(
        _dilated_kernel,
        grid=(b, hg),
        in_specs=specs + [tab, tab],
        out_specs=pl.BlockSpec((1, s, HEAD), lambda i, j: (i, 0, j)),
        out_shape=jax.ShapeDtypeStruct((b, s, hg * HEAD), BF16),
        scratch_shapes=[pltpu.VMEM((s, HEAD), F32)] * 3
        + [pltpu.VMEM((ng, s, HEAD), F32)] * 2
        + [pltpu.VMEM((2 * len(_DIL_WINDOWS), DIL_BLOCK, DIL_BLOCK), F32)],
        compiler_params=_params("parallel", "parallel"),
        name="dilated",
    )(*([proj] * 9), rot_cos, rot_sin)


def _merge_kernel(ret_ref, dil_ref, wr_ref, wd_ref, gr_ref, gd_ref, o_ref):
    a = jnp.dot(ret_ref[...], wr_ref[...], preferred_element_type=F32)
    d = jnp.dot(dil_ref[...], wd_ref[...], preferred_element_type=F32)
    gr = jax.nn.sigmoid(gr_ref[...].astype(F32))
    gd = jax.nn.sigmoid(gd_ref[...].astype(F32))
    o_ref[...] = (gr * a + gd * d).astype(o_ref.dtype)


def _merge(ret, dil, w_br_ret, w_br_dil, proj2d, gate_col0, *, tm, tn):
    m, d_model = ret.shape[0], w_br_ret.shape[1]
    assert gate_col0 % tn == 0 and d_model % tn == 0 and m % tm == 0
    gr0 = gate_col0 // tn
    gd0 = gr0 + d_model // tn
    return pl.pallas_call(
        _merge_kernel,
        grid=(m // tm, d_model // tn),
        in_specs=[
            pl.BlockSpec((tm, ret.shape[1]), lambda i, j: (i, 0)),
            pl.BlockSpec((tm, dil.shape[1]), lambda i, j: (i, 0)),
            pl.BlockSpec((w_br_ret.shape[0], tn), lambda i, j: (0, j)),
            pl.BlockSpec((w_br_dil.shape[0], tn), lambda i, j: (0, j)),
            pl.BlockSpec((tm, tn), lambda i, j: (i, gr0 + j)),
            pl.BlockSpec((tm, tn), lambda i, j: (i, gd0 + j)),
        ],
        out_specs=pl.BlockSpec((tm, tn), lambda i, j: (i, j)),
        out_shape=jax.ShapeDtypeStruct((m, d_model), BF16),
        compiler_params=_params("parallel", "parallel"),
        name="merge",
    )(ret, dil, w_br_ret, w_br_dil, proj2d, proj2d)


def _mem_attn_kernel(q_ref, k_ref, v_ref, o_ref):
    dh = q_ref.shape[2]
    sc = lax.dot_general(q_ref[0], k_ref[0], NT_DIMS, preferred_element_type=F32) * (dh ** -0.5)
    m = jnp.max(sc, axis=-1, keepdims=True)
    p = jnp.exp(sc - m)
    l = jnp.sum(p, axis=-1, keepdims=True)
    o = jnp.dot(p.astype(BF16), v_ref[0], preferred_element_type=F32)
    o_ref[0] = (o / l).astype(o_ref.dtype)


def _mem_attention(q, kv, *, ts):
    b, s, d = q.shape
    m = kv.shape[1]
    dh = d // MEM_HEADS
    return pl.pallas_call(
        _mem_attn_kernel,
        grid=(b, s // ts, MEM_HEADS),
        in_specs=[
            pl.BlockSpec((1, ts, dh), lambda i, j, h: (i, j, h)),
            pl.BlockSpec((1, m, dh), lambda i, j, h: (i, 0, h)),
            pl.BlockSpec((1, m, dh), lambda i, j, h: (i, 0, MEM_HEADS + h)),
        ],
        out_specs=pl.BlockSpec((1, ts, dh), lambda i, j, h: (i, j, h)),
        out_shape=jax.ShapeDtypeStruct((b, s, d), BF16),
        compiler_params=_params("parallel", "parallel", "parallel"),
        name="mem_attention",
    )(q, kv, kv)


NOT_RANKED = 127.0
TOPK_COLUMNS_PER_ITER = 4


def _extract_top(ws, count, tie_break, want_rank=True):
    ws = list(ws)
    n = ws[0].shape[0]
    idx = lax.broadcasted_iota(jnp.int32, ws[0].shape, 0).astype(F32)
    ranks = [jnp.full(w.shape, NOT_RANKED, F32) if want_rank else None for w in ws]
    vals = [[] for _ in ws]
    for it in range(count):
        for p, w in enumerate(ws):
            m = jnp.max(w, axis=0, keepdims=True)
            if tie_break:
                first = jnp.min(jnp.where(w == m, idx, float(n)), axis=0, keepdims=True)
                hit = idx == first
            else:
                hit = w == m
            ws[p] = jnp.where(hit, -jnp.inf, w)
            if want_rank:
                ranks[p] = jnp.where(hit, float(it), ranks[p])
            vals[p].append(m)
    return list(zip(vals, ranks, ws))


_PAIR_CANDIDATES = tuple((a, b) for a in range(PEER_TOPK) for b in range(PEER_TOPK)
                         if (a + 1) * (b + 1) <= PEER_TOPK)


def _peer_topk_kernel(q_ref, sk_ref, r2_ref, e2_ref, n1_ref, e1_ref):
    k = PEER_TOPK
    lanes = 128
    pad_rows = -len(_PAIR_CANDIDATES) % 8

    def columns(cs, tie_break):
        toks = [pl.ds(pl.multiple_of(c * lanes, lanes), lanes) for c in cs]
        sc1s = [lax.dot_general(sk_ref[0, 0], q_ref[tok, :HEAD], NT_DIMS,
                                preferred_element_type=F32) for tok in toks]
        sc2s = [lax.dot_general(sk_ref[0, 1], q_ref[tok, HEAD:], NT_DIMS,
                                preferred_element_type=F32) for tok in toks]
        stage1 = _extract_top(sc1s + sc2s, k, tie_break)
        first, second = stage1[:len(cs)], stage1[len(cs):]
        combos = []
        for (v1, _, _), (v2, _, _) in zip(first, second):
            rows = [v1[a] + v2[b] for a, b in _PAIR_CANDIDATES]
            rows += [jnp.full_like(rows[0], -jnp.inf)] * pad_rows
            combos.append(jnp.concatenate(rows, axis=0))
        stage2 = _extract_top(combos, k, tie_break, want_rank=False)
        off_count = None
        for j, tok in enumerate(toks):
            off = finish(tok, sc1s[j], sc2s[j], first[j], second[j], combos[j], stage2[j])
            off_count = off if off_count is None else off_count + off
        return off_count

    def finish(tok, sc1, sc2, first, second, combo, pairs):
        v1, rank1, _ = first
        v2, rank2, _ = second
        cv, _, left = pairs
        taken = jnp.where(left < combo, 1.0, 0.0)
        z = jnp.zeros_like(cv[0])
        for i in range(k):
            z = z + jnp.exp(cv[i] - cv[0])
        n1 = jnp.zeros_like(sc1)
        for a in range(k):
            rows_a = [r for r, (ca, _) in enumerate(_PAIR_CANDIDATES) if ca == a]
            cnt = taken[rows_a[0]:rows_a[0] + 1, :]
            for r in rows_a[1:]:
                cnt = cnt + taken[r:r + 1, :]
            n1 = jnp.where(rank1 == float(a), cnt, n1)
        sel1 = rank1 < NOT_RANKED
        sel2 = rank2 < NOT_RANKED
        r2_ref[0, :, tok] = rank2.astype(BF16)
        e2_ref[0, :, tok] = jnp.where(sel2, jnp.exp(sc2 - v2[0]), 0.0).astype(BF16)
        n1_ref[0, :, tok] = n1
        e1_ref[0, :, tok] = jnp.where(sel1, jnp.exp(sc1 - v1[0]), 0.0) / z
        count = lambda flags: jnp.sum(flags, axis=0, keepdims=True)
        return (jnp.abs(count(jnp.where(sel1, 1.0, 0.0)) - k) + jnp.abs(count(jnp.where(sel2, 1.0, 0.0)) - k)
                + jnp.abs(count(taken) - k))

    def group(g, _):
        cols = [g * TOPK_COLUMNS_PER_ITER + off for off in range(TOPK_COLUMNS_PER_ITER)]
        off_count = columns(cols, tie_break=False)

        @pl.when(jnp.max(off_count) > 0.0)
        def _():
            columns(cols, tie_break=True)

        return 0

    lax.fori_loop(0, q_ref.shape[0] // (lanes * TOPK_COLUMNS_PER_ITER), group, 0)


def _peer_topk(qp, subkeys, *, tk):
    t = qp.shape[0]
    h, _, nk, dq = subkeys.shape
    spec = pl.BlockSpec((1, nk, tk), lambda i, j: (j, 0, i))
    narrow = jax.ShapeDtypeStruct((h, nk, t), BF16)
    wide = jax.ShapeDtypeStruct((h, nk, t), F32)
    return pl.pallas_call(
        _peer_topk_kernel,
        grid=(t // tk, h),
        in_specs=[pl.BlockSpec((tk, 2 * dq), lambda i, j: (i, j)),
                  pl.BlockSpec((1, 2, nk, dq), lambda i, j: (j, 0, 0, 0))],
        out_specs=[spec] * 4,
        out_shape=[narrow, narrow, wide, wide],
        compiler_params=_params("parallel", "parallel"),
        name="peer_topk",
    )(qp, subkeys)


BF16_SUBLANES = 16


def _row_to_bf16_tile(row, rows):
    tile = jnp.broadcast_to(row, (BF16_SUBLANES, row.shape[1])).astype(BF16)
    return jnp.concatenate([tile] * (rows // BF16_SUBLANES), axis=0)


GATE_ROUNDTRIPS = 3


def _lane_roundtrip(x):
    rows, cols = x.shape
    words = pltpu.bitcast(x, jnp.uint32)
    parts = []
    for c in range(cols // 128):
        part = words[:, c * 128:(c + 1) * 128]
        for _ in range(GATE_ROUNDTRIPS):
            part = pltpu.roll(pltpu.roll(part, 1, 1), 127, 1)
        parts.append(part)
    return pltpu.bitcast(jnp.concatenate(parts, axis=1), x.dtype)


def _peer_ffn_kernel(xn_ref, u_ref, v_ref, r2_ref, e2_ref, n1_ref, e1_ref, res_ref, gf_ref, o_ref,
                     w_ref, *, final_norm):
    j = pl.program_id(1)
    ne = u_ref.shape[0]
    nk = PEER_N_KEYS

    @pl.when(j == 0)
    def _():
        o_ref[...] = jnp.zeros_like(o_ref)

    a = lax.dot_general(u_ref[...], xn_ref[...], NT_DIMS, preferred_element_type=F32)
    for b in range(ne // nk):
        gate = None
        for h in range(PEER_HEADS):
            n1 = _row_to_bf16_tile(n1_ref[h, b:b + 1, :], nk)
            e1 = _row_to_bf16_tile(e1_ref[h, b:b + 1, :], nk)
            term = jnp.where(r2_ref[h] < n1, e2_ref[h], jnp.zeros((), BF16)) * e1
            gate = term if gate is None else gate + term
        gate = _lane_roundtrip(gate)
        ab = a[b * nk:(b + 1) * nk, :]
        act = 0.5 * ab * (1.0 + lax.erf(ab * (0.5 ** 0.5)))
        w_ref[b * nk:(b + 1) * nk, :] = act.astype(BF16) * gate
    o_ref[...] += lax.dot_general(w_ref[...], v_ref[...], TN_DIMS, preferred_element_type=F32)

    @pl.when(j == pl.num_programs(1) - 1)
    def _():
        y = res_ref[...] + o_ref[...]
        if final_norm:
            ms = jnp.mean(y * y, axis=-1, keepdims=True)
            y = (y * lax.rsqrt(ms + NORM_EPS)) * gf_ref[...]
        o_ref[...] = y


def _peer_ffn(xn, u, v, r2, e2, n1, e1, res, g_final, *, tt, ne, final_norm):
    t, d = xn.shape
    n_exp = u.shape[0]
    h, nk, _ = r2.shape
    rows = ne // nk
    assert n_exp == nk * nk and ne % nk == 0 and t % tt == 0 and n_exp % ne == 0 and rows % 8 == 0
    keys = pl.BlockSpec((h, nk, tt), lambda i, j: (0, 0, i))
    first = pl.BlockSpec((h, rows, tt), lambda i, j: (0, j, i))
    return pl.pallas_call(
        functools.partial(_peer_ffn_kernel, final_norm=final_norm),
        grid=(t // tt, n_exp // ne),
        in_specs=[
            pl.BlockSpec((tt, d), lambda i, j: (i, 0)),
            pl.BlockSpec((ne, d), lambda i, j: (j, 0)),
            pl.BlockSpec((ne, d), lambda i, j: (j, 0)),
            keys, keys, first, first,
            pl.BlockSpec((tt, d), lambda i, j: (i, 0)),
            pl.BlockSpec((1, d), lambda i, j: (0, 0)),
        ],
        out_specs=pl.BlockSpec((tt, d), lambda i, j: (i, 0)),
        out_shape=jax.ShapeDtypeStruct((t, d), F32),
        scratch_shapes=[pltpu.VMEM((ne, tt), BF16)],
        compiler_params=_params("parallel", "arbitrary"),
        name="peer_ffn",
    )(xn, u, v, r2, e2, n1, e1, res, g_final.reshape(1, d).astype(F32))


def kernel(x, mem, positions, g_mix, w_in, w_br_ret, w_br_dil, w_out, g_cross, g_mem, w_q_mem,
           w_kv_mem, w_o_mem, g_ffn, w_peer_q, peer_subkeys, peer_u, peer_v, g_final):
    b, s, d = x.shape
    t = b * s
    depth = w_in.shape[0]
    ret_w = RET_HEADS * HEAD
    dil_col0 = 4 * ret_w
    dil_w = len(DIL_PATTERNS) * DIL_HEADS_PER_GROUP * HEAD
    gate_col0 = dil_col0 + 3 * dil_w
    tm = min(1024, t)

    rot_cos, rot_sin, ret_cos, ret_sin = _rotation_tables(positions)
    x2d = x.reshape(t, d)
    for l in range(depth):
        bf = lambda w: w[l].astype(BF16)
        proj = _norm_matmul(x2d, g_mix[l], bf(w_in), tm=tm, tn=2560, out_dtype=BF16)
        proj3 = proj.reshape(b, s, -1)
        ret = _retention(proj3, ret_cos, ret_sin).reshape(t, ret_w)
        dil = _dilated(proj3, rot_cos, rot_sin, dil_col0).reshape(t, -1)
        merged = _merge(ret, dil, bf(w_br_ret), bf(w_br_dil), proj, gate_col0, tm=tm, tn=512)
        x2d = _matmul_res(merged, bf(w_out), x2d, tm=tm, tn=1024)
        mem2d = mem.reshape(-1, d)
        kv = _norm_matmul(mem2d, g_mem[l], bf(w_kv_mem), tm=min(1024, mem2d.shape[0]), tn=1024,
                          out_dtype=BF16)
        qm = _norm_matmul(x2d, g_cross[l], bf(w_q_mem), tm=tm, tn=1024, out_dtype=BF16)
        att = _mem_attention(qm.reshape(b, s, d), kv.reshape(b, -1, 2 * d), ts=min(1024, s))
        x2d = _matmul_res(att.reshape(t, d), bf(w_o_mem), x2d, tm=tm, tn=1024)
        qp, xn = _norm_matmul(x2d, g_ffn[l], bf(w_peer_q), tm=tm, tn=1024, out_dtype=BF16,
                              emit_hn=True)
        r2, e2, n1, e1 = _peer_topk(qp, bf(peer_subkeys), tk=1024)
        x2d = _peer_ffn(xn, bf(peer_u), bf(peer_v), r2, e2, n1, e1, x2d, g_final, tt=512, ne=1024,
                        final_norm=(l == depth - 1))
    return x2d.reshape(b, s, d)
```

```python
import functools
from typing import NamedTuple

import jax
import jax.numpy as jnp
from jax import lax
from jax.experimental import pallas as pl
from jax.experimental.pallas import tpu as pltpu

F32 = jnp.float32
BF16 = jnp.bfloat16

NORM_EPS = 1e-6
ROPE_THETA = 10000.0
HEAD = 128
RET_HEADS = 8
RET_CHUNK = 128
DIL_PATTERNS = ((128, 1), (512, 4), (2048, 16))
DIL_HEADS_PER_GROUP = 4
DIL_BLOCK = 128
MEM_HEADS = 4
PEER_HEADS = 8
PEER_N_KEYS = 128
PEER_TOPK = 16

V7X_VMEM_BYTES = 64 * 1024 * 1024
COMPILER_RESERVE_BYTES = 8 * 1024 * 1024
VMEM_LIMIT = V7X_VMEM_BYTES - COMPILER_RESERVE_BYTES

NT_DIMS = (((1,), (1,)), ((), ()))
TN_DIMS = (((0,), (0,)), ((), ()))


class Tiles(NamedTuple):
    rows: int
    mem_rows: int
    proj_cols: int
    cols: int
    merge_cols: int
    attn_rows: int
    topk_tokens: int
    ffn_tokens: int
    ffn_experts: int


def _tiles(t, s, mem_rows):
    return Tiles(rows=min(1024, t), mem_rows=min(1024, mem_rows), proj_cols=2560, cols=1024,
                 merge_cols=512, attn_rows=min(1024, s), topk_tokens=min(1024, t),
                 ffn_tokens=512, ffn_experts=1024)


def _params(*sem):
    return pltpu.CompilerParams(dimension_semantics=sem, vmem_limit_bytes=VMEM_LIMIT)


def _norm_rows(x_ref, g_ref, hn_ref, rows):
    tm = x_ref.shape[0]

    def body(c, _):
        sl = pl.ds(pl.multiple_of(c * rows, rows), rows)
        x = x_ref[sl, :].astype(F32)
        ms = jnp.mean(x * x, axis=-1, keepdims=True)
        hn_ref[sl, :] = ((x * lax.rsqrt(ms + NORM_EPS)) * g_ref[...]).astype(BF16)
        return 0

    lax.fori_loop(0, tm // rows, body, 0)


def _norm_matmul_kernel(x_ref, g_ref, w_ref, o_ref, hn_ref):
    @pl.when(pl.program_id(1) == 0)
    def _():
        _norm_rows(x_ref, g_ref, hn_ref, 256)

    o_ref[...] = jnp.dot(hn_ref[...], w_ref[...], preferred_element_type=F32).astype(o_ref.dtype)


def _norm_matmul(x, g, w, *, tm, tn, out_dtype, emit_hn=False):
    m, k = x.shape
    n = w.shape[1]
    assert m % tm == 0 and n % tn == 0
    in_specs = [
        pl.BlockSpec((tm, k), lambda i, j: (i, 0)),
        pl.BlockSpec((1, k), lambda i, j: (0, 0)),
        pl.BlockSpec((k, tn), lambda i, j: (0, j)),
    ]
    o_spec = pl.BlockSpec((tm, tn), lambda i, j: (i, j))
    o_shape = jax.ShapeDtypeStruct((m, n), out_dtype)
    if emit_hn:
        out_specs = [o_spec, pl.BlockSpec((tm, k), lambda i, j: (i, 0))]
        out_shape = [o_shape, jax.ShapeDtypeStruct((m, k), BF16)]
        scratch = []
    else:
        out_specs = o_spec
        out_shape = o_shape
        scratch = [pltpu.VMEM((tm, k), BF16)]
    return pl.pallas_call(
        _norm_matmul_kernel,
        grid=(m // tm, n // tn),
        in_specs=in_specs,
        out_specs=out_specs,
        out_shape=out_shape,
        scratch_shapes=scratch,
        compiler_params=_params("parallel", "arbitrary"),
        name="norm_matmul",
    )(x, g.reshape(1, k).astype(F32), w)


def _matmul_res_kernel(a_ref, w_ref, r_ref, o_ref):
    o_ref[...] = r_ref[...] + jnp.dot(a_ref[...], w_ref[...], preferred_element_type=F32)


def _matmul_res(a, w, res, *, tm, tn):
    m, k = a.shape
    n = w.shape[1]
    assert m % tm == 0 and n % tn == 0
    return pl.pallas_call(
        _matmul_res_kernel,
        grid=(m // tm, n // tn),
        in_specs=[
            pl.BlockSpec((tm, k), lambda i, j: (i, 0)),
            pl.BlockSpec((k, tn), lambda i, j: (0, j)),
            pl.BlockSpec((tm, tn), lambda i, j: (i, j)),
        ],
        out_specs=pl.BlockSpec((tm, tn), lambda i, j: (i, j)),
        out_shape=jax.ShapeDtypeStruct((m, n), F32),
        compiler_params=_params("parallel", "parallel"),
        name="matmul_res",
    )(a, w, res)


def _tables_kernel(pos_ref, inv_ref, rc_ref, rs_ref, tc_ref, ts_ref, *, rows):
    s = pos_ref.shape[1]
    lane = lax.broadcasted_iota(jnp.int32, (1, HEAD), 1)
    half_sign = jnp.where(lane < HEAD // 2, -1.0, 1.0).astype(F32)
    pair_sign = jnp.where(lane % 2 == 0, -1.0, 1.0).astype(F32)

    def body(c, _):
        sl = pl.ds(pl.multiple_of(c * rows, rows), rows)
        pos = pos_ref[0, sl, :]
        ang = pos * inv_ref[0:1, :]
        rc_ref[0, sl, :] = jnp.cos(ang)
        rs_ref[0, sl, :] = jnp.sin(ang) * half_sign
        ang = pos * inv_ref[1:2, :]
        tc_ref[0, sl, :] = jnp.cos(ang)
        ts_ref[0, sl, :] = jnp.sin(ang) * pair_sign
        return 0

    lax.fori_loop(0, s // rows, body, 0)


def _rotation_tables(positions):
    b, s = positions.shape
    half = HEAD // 2
    inv_half = 1.0 / (ROPE_THETA ** (jnp.arange(half, dtype=F32) / half))
    inv_pair = 1.0 / (10000.0 ** jnp.linspace(0.0, 1.0, half, dtype=F32))
    inv = jnp.stack([jnp.concatenate([inv_half, inv_half]), jnp.repeat(inv_pair, 2)])
    pos = positions.astype(F32).reshape(b, s, 1)
    tab = jax.ShapeDtypeStruct((b, s, HEAD), F32)
    spec = pl.BlockSpec((1, s, HEAD), lambda i: (i, 0, 0))
    return pl.pallas_call(
        functools.partial(_tables_kernel, rows=256),
        grid=(b,),
        in_specs=[pl.BlockSpec((1, s, 1), lambda i: (i, 0, 0)),
                  pl.BlockSpec((2, HEAD), lambda i: (0, 0))],
        out_specs=[spec] * 4,
        out_shape=[tab] * 4,
        compiler_params=_params("parallel"),
        name="rotation_tables",
    )(pos, inv)


RET_HEADS_PER_STEP = 4


def _retention_kernel(logg_ref, q_ref, k_ref, v_ref, g_ref, cos_ref, sin_ref, o_ref,
                      state_ref, dmat_ref, kw_ref, qw_ref, decay_ref):
    c = RET_CHUNK
    s = q_ref.shape[1]
    hp = RET_HEADS_PER_STEP
    row = lax.broadcasted_iota(jnp.int32, (c, c), 0).astype(F32)
    col = lax.broadcasted_iota(jnp.int32, (c, c), 1).astype(F32)
    diff = row - col
    for h in range(hp):
        lg = logg_ref[pl.program_id(1) * hp + h]
        dmat_ref[h] = jnp.where(diff >= 0, jnp.exp(lg * jnp.maximum(diff, 0.0)), 0.0)
        kw_ref[h] = jnp.exp(lg * (c - 1.0 - row))
        qw_ref[h] = jnp.exp(lg * (row + 1.0))
        decay_ref[h] = jnp.exp(lg * jnp.full((c, c), float(c), F32))
    even = lax.broadcasted_iota(jnp.int32, (c, HEAD), 1) % 2 == 0
    scale = HEAD ** -0.5

    state_ref[...] = jnp.zeros_like(state_ref)

    def body(n, _):
        sl = pl.ds(pl.multiple_of(n * c, c), c)
        cos = cos_ref[0, sl, :]
        sin = sin_ref[0, sl, :]

        def rot(x):
            swapped = jnp.where(even, pltpu.roll(x, HEAD - 1, 1), pltpu.roll(x, 1, 1))
            return x * cos + swapped * sin

        for h in range(hp):
            cols = slice(h * HEAD, (h + 1) * HEAD)
            q = rot(q_ref[0, sl, cols].astype(F32))
            k = rot(k_ref[0, sl, cols].astype(F32)) * scale
            v = v_ref[0, sl, cols]
            scores = lax.dot_general(q.astype(BF16), k.astype(BF16), NT_DIMS,
                                     preferred_element_type=F32) * dmat_ref[h]
            inner = jnp.dot(scores.astype(BF16), v, preferred_element_type=F32)
            state = state_ref[h]
            cross = jnp.dot((q * qw_ref[h]).astype(BF16), state.astype(BF16),
                            preferred_element_type=F32)
            kv = lax.dot_general((k * kw_ref[h]).astype(BF16), v, TN_DIMS,
                                 preferred_element_type=F32)
            state_ref[h] = decay_ref[h] * state + kv
            out = inner + cross
            out = out * lax.rsqrt(jnp.mean(out * out, axis=-1, keepdims=True) + NORM_EPS)
            g = g_ref[0, sl, cols].astype(F32)
            o_ref[0, sl, cols] = (out * (g * jax.nn.sigmoid(g))).astype(o_ref.dtype)
        return 0

    lax.fori_loop(0, s // c, body, 0)


def _retention(proj, ret_cos, ret_sin):
    b, s, _ = proj.shape
    h = RET_HEADS
    hp = RET_HEADS_PER_STEP
    groups = h // hp
    log_g = jnp.log1p(-jnp.exp2(-5.0 - jnp.arange(h, dtype=F32)))
    heads = lambda part: pl.BlockSpec((1, s, hp * HEAD), lambda i, j: (i, 0, part * groups + j))
    tab = pl.BlockSpec((1, s, HEAD), lambda i, j: (i, 0, 0))
    per_head = pltpu.VMEM((hp, HEAD, HEAD), F32)
    return pl.pallas_call(
        _retention_kernel,
        grid=(b, groups),
        in_specs=[pl.BlockSpec(memory_space=pltpu.SMEM),
                  heads(0), heads(1), heads(2), heads(3), tab, tab],
        out_specs=pl.BlockSpec((1, s, hp * HEAD), lambda i, j: (i, 0, j)),
        out_shape=jax.ShapeDtypeStruct((b, s, h * HEAD), BF16),
        scratch_shapes=[per_head] * 5,
        compiler_params=_params("parallel", "parallel"),
        name="retention",
    )(log_g, proj, proj, proj, proj, ret_cos, ret_sin)


_DIL_WINDOWS = tuple(sorted({w // r for w, r in DIL_PATTERNS}))


def _dilated_kernel(*refs):
    qkv_refs = refs[:9]
    cos_ref, sin_ref, o_ref, qs, ks, vs, og, ol, bias_ref = refs[9:]
    s = o_ref.shape[1]
    blk = DIL_BLOCK
    scale = HEAD ** -0.5
    neg = -1e30
    qi = lax.broadcasted_iota(jnp.int32, (blk, blk), 0)
    kj = lax.broadcasted_iota(jnp.int32, (blk, blk), 1)
    cos = cos_ref[0]
    sin = sin_ref[0]

    def rot(x):
        return x * cos + pltpu.roll(x, HEAD // 2, 1) * sin

    for wi, win in enumerate(_DIL_WINDOWS):
        cur_ok = (qi - kj >= 0) & (qi - kj <= win)
        prev_ok = (qi - kj + blk) <= win
        bias_ref[2 * wi] = jnp.where(cur_ok, 0.0, neg)
        bias_ref[2 * wi + 1] = jnp.where(prev_ok, 0.0, neg)

    for g, (window, r) in enumerate(DIL_PATTERNS):
        wi = _DIL_WINDOWS.index(window // r)
        length = s // r
        nb = length // blk
        q_ref, k_ref, v_ref = qkv_refs[3 * g:3 * g + 3]
        qs[...] = rot(q_ref[0].astype(F32)) * scale
        ks[...] = rot(k_ref[0].astype(F32))
        vs[...] = v_ref[0].astype(F32)
        units = [(c, n) for c in range(r) for n in range(nb)]

        def rows(c, m):
            return pl.ds(c + m * blk * r, blk, stride=r) if r > 1 else pl.ds(m * blk, blk)

        def scores(c, n):
            q = qs[rows(c, n), :].astype(BF16)
            k1 = ks[rows(c, n), :].astype(BF16)
            s1 = lax.dot_general(q, k1, NT_DIMS, preferred_element_type=F32) + bias_ref[2 * wi]
            s0 = None
            if n > 0:
                k0 = ks[rows(c, n - 1), :].astype(BF16)
                s0 = (lax.dot_general(q, k0, NT_DIMS, preferred_element_type=F32)
                      + bias_ref[2 * wi + 1])
            return s1, s0

        def softmax(s1, s0):
            m = jnp.max(s1, axis=-1, keepdims=True)
            if s0 is not None:
                m = jnp.maximum(m, jnp.max(s0, axis=-1, keepdims=True))
            p1 = jnp.exp(s1 - m)
            l = jnp.sum(p1, axis=-1, keepdims=True)
            p0 = None
            if s0 is not None:
                p0 = jnp.exp(s0 - m)
                l = l + jnp.sum(p0, axis=-1, keepdims=True)
                p0 = p0.astype(BF16)
            return p1.astype(BF16), p0, m, l

        def finish(c, n, p1, p0, m, l):
            acc = jnp.dot(p1, vs[rows(c, n), :].astype(BF16), preferred_element_type=F32)
            if p0 is not None:
                acc = acc + jnp.dot(p0, vs[rows(c, n - 1), :].astype(BF16),
                                    preferred_element_type=F32)
            og[g, rows(c, n), :] = acc / l
            ol[g, rows(c, n), :] = jnp.broadcast_to(m + jnp.log(l), (blk, HEAD))

        sc, sm = {}, {}
        for i in range(len(units) + 2):
            if i < len(units):
                sc[i] = scores(*units[i])
            if 1 <= i <= len(units):
                sm[i - 1] = softmax(*sc.pop(i - 1))
            if i >= 2:
                finish(*units[i - 2], *sm.pop(i - 2))

    ng = len(DIL_PATTERNS)
    mx = ol[0]
    for g in range(1, ng):
        mx = jnp.maximum(mx, ol[g])
    num = jnp.zeros((s, HEAD), F32)
    den = jnp.zeros((s, HEAD), F32)
    for g in range(ng):
        e = jnp.exp(ol[g] - mx)
        num = num + e * og[g]
        den = den + e
    o_ref[0] = (num / den).astype(o_ref.dtype)


def _dilated(proj, rot_cos, rot_sin, col0):
    b, s, _ = proj.shape
    assert s % (DIL_BLOCK * max(r for _, r in DIL_PATTERNS)) == 0
    hg = DIL_HEADS_PER_GROUP
    ng = len(DIL_PATTERNS)
    width = ng * hg
    base = col0 // HEAD
    specs = []
    for g in range(ng):
        for part in range(3):
            off = base + part * width + g * hg
            specs.append(pl.BlockSpec((1, s, HEAD), lambda i, j, off=off: (i, 0, off + j)))
    tab = pl.BlockSpec((1, s, HEAD), lambda i, j: (i, 0, 0))
    return pl.pallas_call(
        _dilated_kernel,
        grid=(b, hg),
        in_specs=specs + [tab, tab],
        out_specs=pl.BlockSpec((1, s, HEAD), lambda i, j: (i, 0, j)),
        out_shape=jax.ShapeDtypeStruct((b, s, hg * HEAD), BF16),
        scratch_shapes=[pltpu.VMEM((s, HEAD), F32)] * 3
        + [pltpu.VMEM((ng, s, HEAD), F32)] * 2
        + [pltpu.VMEM((2 * len(_DIL_WINDOWS), DIL_BLOCK, DIL_BLOCK), F32)],
        compiler_params=_params("parallel", "parallel"),
        name="dilated",
    )(*([proj] * 9), rot_cos, rot_sin)


def _merge_kernel(ret_ref, dil_ref, wr_ref, wd_ref, gr_ref, gd_ref, o_ref):
    a = jnp.dot(ret_ref[...], wr_ref[...], preferred_element_type=F32)
    d = jnp.dot(dil_ref[...], wd_ref[...], preferred_element_type=F32)
    gr = jax.nn.sigmoid(gr_ref[...].astype(F32))
    gd = jax.nn.sigmoid(gd_ref[...].astype(F32))
    o_ref[...] = (gr * a + gd * d).astype(o_ref.dtype)


def _merge(ret, dil, w_br_ret, w_br_dil, proj2d, gate_col0, *, tm, tn):
    m, d_model = ret.shape[0], w_br_ret.shape[1]
    assert gate_col0 % tn == 0 and d_model % tn == 0 and m % tm == 0
    gr0 = gate_col0 // tn
    gd0 = gr0 + d_model // tn
    return pl.pallas_call(
        _merge_kernel,
        grid=(m // tm, d_model // tn),
        in_specs=[
            pl.BlockSpec((tm, ret.shape[1]), lambda i, j: (i, 0)),
            pl.BlockSpec((tm, dil.shape[1]), lambda i, j: (i, 0)),
            pl.BlockSpec((w_br_ret.shape[0], tn), lambda i, j: (0, j)),
            pl.BlockSpec((w_br_dil.shape[0], tn), lambda i, j: (0, j)),
            pl.BlockSpec((tm, tn), lambda i, j: (i, gr0 + j)),
            pl.BlockSpec((tm, tn), lambda i, j: (i, gd0 + j)),
        ],
        out_specs=pl.BlockSpec((tm, tn), lambda i, j: (i, j)),
        out_shape=jax.ShapeDtypeStruct((m, d_model), BF16),
        compiler_params=_params("parallel", "parallel"),
        name="merge",
    )(ret, dil, w_br_ret, w_br_dil, proj2d, proj2d)


def _mem_attn_kernel(q_ref, k_ref, v_ref, o_ref):
    dh = q_ref.shape[2]
    sc = lax.dot_general(q_ref[0], k_ref[0], NT_DIMS, preferred_element_type=F32) * (dh ** -0.5)
    m = jnp.max(sc, axis=-1, keepdims=True)
    p = jnp.exp(sc - m)
    l = jnp.sum(p, axis=-1, keepdims=True)
    o = jnp.dot(p.astype(BF16), v_ref[0], preferred_element_type=F32)
    o_ref[0] = (o / l).astype(o_ref.dtype)


def _mem_attention(q, kv, *, ts):
    b, s, d = q.shape
    m = kv.shape[1]
    dh = d // MEM_HEADS
    return pl.pallas_call(
        _mem_attn_kernel,
        grid=(b, s // ts, MEM_HEADS),
        in_specs=[
            pl.BlockSpec((1, ts, dh), lambda i, j, h: (i, j, h)),
            pl.BlockSpec((1, m, dh), lambda i, j, h: (i, 0, h)),
            pl.BlockSpec((1, m, dh), lambda i, j, h: (i, 0, MEM_HEADS + h)),
        ],
        out_specs=pl.BlockSpec((1, ts, dh), lambda i, j, h: (i, j, h)),
        out_shape=jax.ShapeDtypeStruct((b, s, d), BF16),
        compiler_params=_params("parallel", "parallel", "parallel"),
        name="mem_attention",
    )(q, kv, kv)


NOT_RANKED = 127.0
TOPK_COLUMNS_PER_ITER = 4


def _extract_top(ws, count, tie_break, want_rank=True):
    ws = list(ws)
    n = ws[0].shape[0]
    idx = lax.broadcasted_iota(jnp.int32, ws[0].shape, 0).astype(F32)
    ranks = [jnp.full(w.shape, NOT_RANKED, F32) if want_rank else None for w in ws]
    vals = [[] for _ in ws]
    for it in range(count):
        for p, w in enumerate(ws):
            m = jnp.max(w, axis=0, keepdims=True)
            if tie_break:
                first = jnp.min(jnp.where(w == m, idx, float(n)), axis=0, keepdims=True)
                hit = idx == first
            else:
                hit = w == m
            ws[p] = jnp.where(hit, -jnp.inf, w)
            if want_rank:
                ranks[p] = jnp.where(hit, float(it), ranks[p])
            vals[p].append(m)
    return list(zip(vals, ranks, ws))


_PAIR_CANDIDATES = tuple((a, b) for a in range(PEER_TOPK) for b in range(PEER_TOPK)
                         if (a + 1) * (b + 1) <= PEER_TOPK)


def _peer_topk_kernel(q_ref, sk_ref, r2_ref, e2_ref, n1_ref, e1_ref):
    k = PEER_TOPK
    lanes = 128
    pad_rows = -len(_PAIR_CANDIDATES) % 8

    def columns(cs, tie_break):
        toks = [pl.ds(pl.multiple_of(c * lanes, lanes), lanes) for c in cs]
        sc1s = [lax.dot_general(sk_ref[0, 0], q_ref[tok, :HEAD], NT_DIMS,
                                preferred_element_type=F32) for tok in toks]
        sc2s = [lax.dot_general(sk_ref[0, 1], q_ref[tok, HEAD:], NT_DIMS,
                                preferred_element_type=F32) for tok in toks]
        stage1 = _extract_top(sc1s + sc2s, k, tie_break)
        first, second = stage1[:len(cs)], stage1[len(cs):]
        combos = []
        for (v1, _, _), (v2, _, _) in zip(first, second):
            rows = [v1[a] + v2[b] for a, b in _PAIR_CANDIDATES]
            rows += [jnp.full_like(rows[0], -jnp.inf)] * pad_rows
            combos.append(jnp.concatenate(rows, axis=0))
        stage2 = _extract_top(combos, k, tie_break, want_rank=False)
        off_count = None
        for j, tok in enumerate(toks):
            off = finish(tok, sc1s[j], sc2s[j], first[j], second[j], combos[j], stage2[j])
            off_count = off if off_count is None else off_count + off
        return off_count

    def finish(tok, sc1, sc2, first, second, combo, pairs):
        v1, rank1, _ = first
        v2, rank2, _ = second
        cv, _, left = pairs
        taken = jnp.where(left < combo, 1.0, 0.0)
        z = jnp.zeros_like(cv[0])
        for i in range(k):
            z = z + jnp.exp(cv[i] - cv[0])
        n1 = jnp.zeros_like(sc1)
        for a in range(k):
            rows_a = [r for r, (ca, _) in enumerate(_PAIR_CANDIDATES) if ca == a]
            cnt = taken[rows_a[0]:rows_a[0] + 1, :]
            for r in rows_a[1:]:
                cnt = cnt + taken[r:r + 1, :]
            n1 = jnp.where(rank1 == float(a), cnt, n1)
        sel1 = rank1 < NOT_RANKED
        sel2 = rank2 < NOT_RANKED
        r2_ref[0, :, tok] = rank2.astype(BF16)
        e2_ref[0, :, tok] = jnp.where(sel2, jnp.exp(sc2 - v2[0]), 0.0).astype(BF16)
        n1_ref[0, :, tok] = n1
        e1_ref[0, :, tok] = jnp.where(sel1, jnp.exp(sc1 - v1[0]), 0.0) / z
        count = lambda flags: jnp.sum(flags, axis=0, keepdims=True)
        return (jnp.abs(count(jnp.where(sel1, 1.0, 0.0)) - k) + jnp.abs(count(jnp.where(sel2, 1.0, 0.0)) - k)
                + jnp.abs(count(taken) - k))

    def group(g, _):
        cols = [g * TOPK_COLUMNS_PER_ITER + off for off in range(TOPK_COLUMNS_PER_ITER)]
        off_count = columns(cols, tie_break=False)

        @pl.when(jnp.max(off_count) > 0.0)
        def _():
            columns(cols, tie_break=True)

        return 0

    lax.fori_loop(0, q_ref.shape[0] // (lanes * TOPK_COLUMNS_PER_ITER), group, 0)


def _peer_topk(qp, subkeys, *, tk):
    t = qp.shape[0]
    h, _, nk, dq = subkeys.shape
    spec = pl.BlockSpec((1, nk, tk), lambda i, j: (j, 0, i))
    narrow = jax.ShapeDtypeStruct((h, nk, t), BF16)
    wide = jax.ShapeDtypeStruct((h, nk, t), F32)
    return pl.pallas_call(
        _peer_topk_kernel,
        grid=(t // tk, h),
        in_specs=[pl.BlockSpec((tk, 2 * dq), lambda i, j: (i, j)),
                  pl.BlockSpec((1, 2, nk, dq), lambda i, j: (j, 0, 0, 0))],
        out_specs=[spec] * 4,
        out_shape=[narrow, narrow, wide, wide],
        compiler_params=_params("parallel", "parallel"),
        name="peer_topk",
    )(qp, subkeys)


BF16_SUBLANES = 16


def _row_to_bf16_tile(row, rows):
    tile = jnp.broadcast_to(row, (BF16_SUBLANES, row.shape[1])).astype(BF16)
    return jnp.concatenate([tile] * (rows // BF16_SUBLANES), axis=0)


GATE_ROUNDTRIPS = 3


def _lane_roundtrip(x):
    rows, cols = x.shape
    words = pltpu.bitcast(x, jnp.uint32)
    parts = []
    for c in range(cols // 128):
        part = words[:, c * 128:(c + 1) * 128]
        for _ in range(GATE_ROUNDTRIPS):
            part = pltpu.roll(pltpu.roll(part, 1, 1), 127, 1)
        parts.append(part)
    return pltpu.bitcast(jnp.concatenate(parts, axis=1), x.dtype)


def _peer_ffn_kernel(xn_ref, u_ref, v_ref, r2_ref, e2_ref, n1_ref, e1_ref, res_ref, gf_ref, o_ref,
                     w_ref, *, final_norm):
    j = pl.program_id(1)
    ne = u_ref.shape[0]
    nk = PEER_N_KEYS

    @pl.when(j == 0)
    def _():
        o_ref[...] = jnp.zeros_like(o_ref)

    a = lax.dot_general(u_ref[...], xn_ref[...], NT_DIMS, preferred_element_type=F32)
    for b in range(ne // nk):
        gate = None
        for h in range(PEER_HEADS):
            n1 = _row_to_bf16_tile(n1_ref[h, b:b + 1, :], nk)
            e1 = _row_to_bf16_tile(e1_ref[h, b:b + 1, :], nk)
            term = jnp.where(r2_ref[h] < n1, e2_ref[h], jnp.zeros((), BF16)) * e1
            gate = term if gate is None else gate + term
        gate = _lane_roundtrip(gate)
        ab = a[b * nk:(b + 1) * nk, :]
        act = 0.5 * ab * (1.0 + lax.erf(ab * (0.5 ** 0.5)))
        w_ref[b * nk:(b + 1) * nk, :] = act.astype(BF16) * gate
    o_ref[...] += lax.dot_general(w_ref[...], v_ref[...], TN_DIMS, preferred_element_type=F32)

    @pl.when(j == pl.num_programs(1) - 1)
    def _():
        y = res_ref[...] + o_ref[...]
        if final_norm:
            ms = jnp.mean(y * y, axis=-1, keepdims=True)
            y = (y * lax.rsqrt(ms + NORM_EPS)) * gf_ref[...]
        o_ref[...] = y


def _peer_ffn(xn, u, v, r2, e2, n1, e1, res, g_final, *, tt, ne, final_norm):
    t, d = xn.shape
    n_exp = u.shape[0]
    h, nk, _ = r2.shape
    rows = ne // nk
    assert n_exp == nk * nk and ne % nk == 0 and t % tt == 0 and n_exp % ne == 0 and rows % 8 == 0
    keys = pl.BlockSpec((h, nk, tt), lambda i, j: (0, 0, i))
    first = pl.BlockSpec((h, rows, tt), lambda i, j: (0, j, i))
    return pl.pallas_call(
        functools.partial(_peer_ffn_kernel, final_norm=final_norm),
        grid=(t // tt, n_exp // ne),
        in_specs=[
            pl.BlockSpec((tt, d), lambda i, j: (i, 0)),
            pl.BlockSpec((ne, d), lambda i, j: (j, 0)),
            pl.BlockSpec((ne, d), lambda i, j: (j, 0)),
            keys, keys, first, first,
            pl.BlockSpec((tt, d), lambda i, j: (i, 0)),
            pl.BlockSpec((1, d), lambda i, j: (0, 0)),
        ],
        out_specs=pl.BlockSpec((tt, d), lambda i, j: (i, 0)),
        out_shape=jax.ShapeDtypeStruct((t, d), F32),
        scratch_shapes=[pltpu.VMEM((ne, tt), BF16)],
        compiler_params=_params("parallel", "arbitrary"),
        name="peer_ffn",
    )(xn, u, v, r2, e2, n1, e1, res, g_final.reshape(1, d).astype(F32))


def kernel(x, mem, positions, g_mix, w_in, w_br_ret, w_br_dil, w_out, g_cross, g_mem, w_q_mem,
           w_kv_mem, w_o_mem, g_ffn, w_peer_q, peer_subkeys, peer_u, peer_v, g_final):
    b, s, d = x.shape
    t = b * s
    depth = w_in.shape[0]
    ret_w = RET_HEADS * HEAD
    dil_col0 = 4 * ret_w
    dil_w = len(DIL_PATTERNS) * DIL_HEADS_PER_GROUP * HEAD
    gate_col0 = dil_col0 + 3 * dil_w
    mem2d = mem.reshape(-1, d)
    tl = _tiles(t, s, mem2d.shape[0])

    rot_cos, rot_sin, ret_cos, ret_sin = _rotation_tables(positions)
    x2d = x.reshape(t, d)
    for l in range(depth):
        bf = lambda w: w[l].astype(BF16)
        proj = _norm_matmul(x2d, g_mix[l], bf(w_in), tm=tl.rows, tn=tl.proj_cols, out_dtype=BF16)
        proj3 = proj.reshape(b, s, -1)
        ret = _retention(proj3, ret_cos, ret_sin).reshape(t, ret_w)
        dil = _dilated(proj3, rot_cos, rot_sin, dil_col0).reshape(t, -1)
        merged = _merge(ret, dil, bf(w_br_ret), bf(w_br_dil), proj, gate_col0, tm=tl.rows,
                        tn=tl.merge_cols)
        x2d = _matmul_res(merged, bf(w_out), x2d, tm=tl.rows, tn=tl.cols)
        kv = _norm_matmul(mem2d, g_mem[l], bf(w_kv_mem), tm=tl.mem_rows, tn=tl.cols, out_dtype=BF16)
        qm = _norm_matmul(x2d, g_cross[l], bf(w_q_mem), tm=tl.rows, tn=tl.cols, out_dtype=BF16)
        att = _mem_attention(qm.reshape(b, s, d), kv.reshape(b, -1, 2 * d), ts=tl.attn_rows)
        x2d = _matmul_res(att.reshape(t, d), bf(w_o_mem), x2d, tm=tl.rows, tn=tl.cols)
        qp, xn = _norm_matmul(x2d, g_ffn[l], bf(w_peer_q), tm=tl.rows, tn=tl.cols, out_dtype=BF16,
                              emit_hn=True)
        r2, e2, n1, e1 = _peer_topk(qp, bf(peer_subkeys), tk=tl.topk_tokens)
        x2d = _peer_ffn(xn, bf(peer_u), bf(peer_v), r2, e2, n1, e1, x2d, g_final,
                        tt=tl.ffn_tokens, ne=tl.ffn_experts, final_norm=(l == depth - 1))
    return x2d.reshape(b, s, d)
```

```python
import functools
from typing import NamedTuple

import jax
import jax.numpy as jnp
from jax import lax
from jax.experimental import pallas as pl
from jax.experimental.pallas import tpu as pltpu

F32 = jnp.float32
BF16 = jnp.bfloat16

NORM_EPS = 1e-6
ROPE_THETA = 10000.0
HEAD = 128
RET_HEADS = 8
RET_CHUNK = 128
DIL_PATTERNS = ((128, 1), (512, 4), (2048, 16))
DIL_HEADS_PER_GROUP = 4
DIL_BLOCK = 128
MEM_HEADS = 4
PEER_HEADS = 8
PEER_N_KEYS = 128
PEER_TOPK = 16

V7X_VMEM_BYTES = 64 * 1024 * 1024
COMPILER_RESERVE_BYTES = 8 * 1024 * 1024
VMEM_LIMIT = V7X_VMEM_BYTES - COMPILER_RESERVE_BYTES

NT_DIMS = (((1,), (1,)), ((), ()))
TN_DIMS = (((0,), (0,)), ((), ()))


class Tiles(NamedTuple):
    rows: int
    mem_rows: int
    proj_cols: int
    cols: int
    merge_cols: int
    attn_rows: int
    topk_tokens: int
    ffn_tokens: int
    ffn_experts: int


def _tiles(t, s, mem_rows):
    return Tiles(rows=min(1024, t), mem_rows=min(1024, mem_rows), proj_cols=2560, cols=1024,
                 merge_cols=512, attn_rows=min(1024, s), topk_tokens=min(1024, t),
                 ffn_tokens=512, ffn_experts=1024)


def _params(*sem):
    return pltpu.CompilerParams(dimension_semantics=sem, vmem_limit_bytes=VMEM_LIMIT)


def _norm_rows(x_ref, g_ref, hn_ref, rows):
    tm = x_ref.shape[0]

    def body(c, _):
        sl = pl.ds(pl.multiple_of(c * rows, rows), rows)
        x = x_ref[sl, :].astype(F32)
        ms = jnp.mean(x * x, axis=-1, keepdims=True)
        hn_ref[sl, :] = ((x * lax.rsqrt(ms + NORM_EPS)) * g_ref[...]).astype(BF16)
        return 0

    lax.fori_loop(0, tm // rows, body, 0)


def _norm_matmul_kernel(x_ref, g_ref, w_ref, o_ref, hn_ref):
    @pl.when(pl.program_id(1) == 0)
    def _():
        _norm_rows(x_ref, g_ref, hn_ref, 256)

    o_ref[...] = jnp.dot(hn_ref[...], w_ref[...], preferred_element_type=F32).astype(o_ref.dtype)


def _norm_matmul(x, g, w, *, tm, tn, out_dtype, emit_hn=False):
    m, k = x.shape
    n = w.shape[1]
    assert m % tm == 0 and n % tn == 0
    in_specs = [
        pl.BlockSpec((tm, k), lambda i, j: (i, 0)),
        pl.BlockSpec((1, k), lambda i, j: (0, 0)),
        pl.BlockSpec((k, tn), lambda i, j: (0, j)),
    ]
    o_spec = pl.BlockSpec((tm, tn), lambda i, j: (i, j))
    o_shape = jax.ShapeDtypeStruct((m, n), out_dtype)
    if emit_hn:
        out_specs = [o_spec, pl.BlockSpec((tm, k), lambda i, j: (i, 0))]
        out_shape = [o_shape, jax.ShapeDtypeStruct((m, k), BF16)]
        scratch = []
    else:
        out_specs = o_spec
        out_shape = o_shape
        scratch = [pltpu.VMEM((tm, k), BF16)]
    return pl.pallas_call(
        _norm_matmul_kernel,
        grid=(m // tm, n // tn),
        in_specs=in_specs,
        out_specs=out_specs,
        out_shape=out_shape,
        scratch_shapes=scratch,
        compiler_params=_params("parallel", "arbitrary"),
        name="norm_matmul",
    )(x, g.reshape(1, k).astype(F32), w)


def _matmul_res_kernel(a_ref, w_ref, r_ref, o_ref):
    o_ref[...] = r_ref[...] + jnp.dot(a_ref[...], w_ref[...], preferred_element_type=F32)


def _matmul_res(a, w, res, *, tm, tn):
    m, k = a.shape
    n = w.shape[1]
    assert m % tm == 0 and n % tn == 0
    return pl.pallas_call(
        _matmul_res_kernel,
        grid=(m // tm, n // tn),
        in_specs=[
            pl.BlockSpec((tm, k), lambda i, j: (i, 0)),
            pl.BlockSpec((k, tn), lambda i, j: (0, j)),
            pl.BlockSpec((tm, tn), lambda i, j: (i, j)),
        ],
        out_specs=pl.BlockSpec((tm, tn), lambda i, j: (i, j)),
        out_shape=jax.ShapeDtypeStruct((m, n), F32),
        compiler_params=_params("parallel", "parallel"),
        name="matmul_res",
    )(a, w, res)


def _tables_kernel(pos_ref, inv_ref, rc_ref, rs_ref, tc_ref, ts_ref, *, rows):
    s = pos_ref.shape[1]
    lane = lax.broadcasted_iota(jnp.int32, (1, HEAD), 1)
    half_sign = jnp.where(lane < HEAD // 2, -1.0, 1.0).astype(F32)
    pair_sign = jnp.where(lane % 2 == 0, -1.0, 1.0).astype(F32)

    def body(c, _):
        sl = pl.ds(pl.multiple_of(c * rows, rows), rows)
        pos = pos_ref[0, sl, :]
        ang = pos * inv_ref[0:1, :]
        rc_ref[0, sl, :] = jnp.cos(ang)
        rs_ref[0, sl, :] = jnp.sin(ang) * half_sign
        ang = pos * inv_ref[1:2, :]
        tc_ref[0, sl, :] = jnp.cos(ang)
        ts_ref[0, sl, :] = jnp.sin(ang) * pair_sign
        return 0

    lax.fori_loop(0, s // rows, body, 0)


def _rotation_tables(positions):
    b, s = positions.shape
    half = HEAD // 2
    inv_half = 1.0 / (ROPE_THETA ** (jnp.arange(half, dtype=F32) / half))
    inv_pair = 1.0 / (10000.0 ** jnp.linspace(0.0, 1.0, half, dtype=F32))
    inv = jnp.stack([jnp.concatenate([inv_half, inv_half]), jnp.repeat(inv_pair, 2)])
    pos = positions.astype(F32).reshape(b, s, 1)
    tab = jax.ShapeDtypeStruct((b, s, HEAD), F32)
    spec = pl.BlockSpec((1, s, HEAD), lambda i: (i, 0, 0))
    return pl.pallas_call(
        functools.partial(_tables_kernel, rows=256),
        grid=(b,),
        in_specs=[pl.BlockSpec((1, s, 1), lambda i: (i, 0, 0)),
                  pl.BlockSpec((2, HEAD), lambda i: (0, 0))],
        out_specs=[spec] * 4,
        out_shape=[tab] * 4,
        compiler_params=_params("parallel"),
        name="rotation_tables",
    )(pos, inv)


RET_HEADS_PER_STEP = 8


def _retention_kernel(logg_ref, q_ref, k_ref, v_ref, g_ref, cos_ref, sin_ref, o_ref,
                      state_ref, dmat_ref, kw_ref, qw_ref, decay_ref):
    c = RET_CHUNK
    s = q_ref.shape[1]
    hp = RET_HEADS_PER_STEP
    row = lax.broadcasted_iota(jnp.int32, (c, c), 0).astype(F32)
    col = lax.broadcasted_iota(jnp.int32, (c, c), 1).astype(F32)
    diff = row - col
    for h in range(hp):
        lg = logg_ref[pl.program_id(1) * hp + h]
        dmat_ref[h] = jnp.where(diff >= 0, jnp.exp(lg * jnp.maximum(diff, 0.0)), 0.0)
        kw_ref[h] = jnp.exp(lg * (c - 1.0 - row))
        qw_ref[h] = jnp.exp(lg * (row + 1.0))
        decay_ref[h] = jnp.exp(lg * jnp.full((c, c), float(c), F32))
    even = lax.broadcasted_iota(jnp.int32, (c, HEAD), 1) % 2 == 0
    scale = HEAD ** -0.5

    state_ref[...] = jnp.zeros_like(state_ref)

    def body(n, _):
        sl = pl.ds(pl.multiple_of(n * c, c), c)
        cos = cos_ref[0, sl, :]
        sin = sin_ref[0, sl, :]

        def rot(x):
            swapped = jnp.where(even, pltpu.roll(x, HEAD - 1, 1), pltpu.roll(x, 1, 1))
            return x * cos + swapped * sin

        heads = range(hp)
        cols = [slice(h * HEAD, (h + 1) * HEAD) for h in heads]
        q = [rot(q_ref[0, sl, cols[h]].astype(F32)) for h in heads]
        k = [rot(k_ref[0, sl, cols[h]].astype(F32)) * scale for h in heads]
        v = [v_ref[0, sl, cols[h]] for h in heads]
        state = [state_ref[h] for h in heads]
        scores = [lax.dot_general(q[h].astype(BF16), k[h].astype(BF16), NT_DIMS,
                                  preferred_element_type=F32) * dmat_ref[h] for h in heads]
        cross = [jnp.dot((q[h] * qw_ref[h]).astype(BF16), state[h].astype(BF16),
                         preferred_element_type=F32) for h in heads]
        kv = [lax.dot_general((k[h] * kw_ref[h]).astype(BF16), v[h], TN_DIMS,
                              preferred_element_type=F32) for h in heads]
        inner = [jnp.dot(scores[h].astype(BF16), v[h], preferred_element_type=F32) for h in heads]
        for h in heads:
            state_ref[h] = decay_ref[h] * state[h] + kv[h]
            out = inner[h] + cross[h]
            out = out * lax.rsqrt(jnp.mean(out * out, axis=-1, keepdims=True) + NORM_EPS)
            g = g_ref[0, sl, cols[h]].astype(F32)
            o_ref[0, sl, cols[h]] = (out * (g * jax.nn.sigmoid(g))).astype(o_ref.dtype)
        return 0

    lax.fori_loop(0, s // c, body, 0)


def _retention(proj, ret_cos, ret_sin):
    b, s, _ = proj.shape
    h = RET_HEADS
    hp = RET_HEADS_PER_STEP
    groups = h // hp
    log_g = jnp.log1p(-jnp.exp2(-5.0 - jnp.arange(h, dtype=F32)))
    heads = lambda part: pl.BlockSpec((1, s, hp * HEAD), lambda i, j: (i, 0, part * groups + j))
    tab = pl.BlockSpec((1, s, HEAD), lambda i, j: (i, 0, 0))
    per_head = pltpu.VMEM((hp, HEAD, HEAD), F32)
    return pl.pallas_call(
        _retention_kernel,
        grid=(b, groups),
        in_specs=[pl.BlockSpec(memory_space=pltpu.SMEM),
                  heads(0), heads(1), heads(2), heads(3), tab, tab],
        out_specs=pl.BlockSpec((1, s, hp * HEAD), lambda i, j: (i, 0, j)),
        out_shape=jax.ShapeDtypeStruct((b, s, h * HEAD), BF16),
        scratch_shapes=[per_head] * 5,
        compiler_params=_params("parallel", "parallel"),
        name="retention",
    )(log_g, proj, proj, proj, proj, ret_cos, ret_sin)


_DIL_WINDOWS = tuple(sorted({w // r for w, r in DIL_PATTERNS}))


def _dilated_kernel(*refs):
    qkv_refs = refs[:9]
    cos_ref, sin_ref, o_ref, qs, ks, vs, og, ol, bias_ref = refs[9:]
    s = o_ref.shape[1]
    blk = DIL_BLOCK
    scale = HEAD ** -0.5
    neg = -1e30
    qi = lax.broadcasted_iota(jnp.int32, (blk, blk), 0)
    kj = lax.broadcasted_iota(jnp.int32, (blk, blk), 1)
    cos = cos_ref[0]
    sin = sin_ref[0]

    def rot(x):
        return x * cos + pltpu.roll(x, HEAD // 2, 1) * sin

    for wi, win in enumerate(_DIL_WINDOWS):
        cur_ok = (qi - kj >= 0) & (qi - kj <= win)
        prev_ok = (qi - kj + blk) <= win
        bias_ref[2 * wi] = jnp.where(cur_ok, 0.0, neg)
        bias_ref[2 * wi + 1] = jnp.where(prev_ok, 0.0, neg)

    for g, (window, r) in enumerate(DIL_PATTERNS):
        wi = _DIL_WINDOWS.index(window // r)
        length = s // r
        nb = length // blk
        q_ref, k_ref, v_ref = qkv_refs[3 * g:3 * g + 3]
        qs[...] = rot(q_ref[0].astype(F32)) * scale
        ks[...] = rot(k_ref[0].astype(F32))
        vs[...] = v_ref[0].astype(F32)
        units = [(c, n) for c in range(r) for n in range(nb)]

        def rows(c, m):
            return pl.ds(c + m * blk * r, blk, stride=r) if r > 1 else pl.ds(m * blk, blk)

        def scores(c, n):
            q = qs[rows(c, n), :].astype(BF16)
            k1 = ks[rows(c, n), :].astype(BF16)
            s1 = lax.dot_general(q, k1, NT_DIMS, preferred_element_type=F32) + bias_ref[2 * wi]
            s0 = None
            if n > 0:
                k0 = ks[rows(c, n - 1), :].astype(BF16)
                s0 = (lax.dot_general(q, k0, NT_DIMS, preferred_element_type=F32)
                      + bias_ref[2 * wi + 1])
            return s1, s0

        def softmax(s1, s0):
            m = jnp.max(s1, axis=-1, keepdims=True)
            if s0 is not None:
                m = jnp.maximum(m, jnp.max(s0, axis=-1, keepdims=True))
            p1 = jnp.exp(s1 - m)
            l = jnp.sum(p1, axis=-1, keepdims=True)
            p0 = None
            if s0 is not None:
                p0 = jnp.exp(s0 - m)
                l = l + jnp.sum(p0, axis=-1, keepdims=True)
                p0 = p0.astype(BF16)
            return p1.astype(BF16), p0, m, l

        def finish(c, n, p1, p0, m, l):
            acc = jnp.dot(p1, vs[rows(c, n), :].astype(BF16), preferred_element_type=F32)
            if p0 is not None:
                acc = acc + jnp.dot(p0, vs[rows(c, n - 1), :].astype(BF16),
                                    preferred_element_type=F32)
            og[g, rows(c, n), :] = acc / l
            ol[g, rows(c, n), :] = jnp.broadcast_to(m + jnp.log(l), (blk, HEAD))

        sc, sm = {}, {}
        for i in range(len(units) + 2):
            if i < len(units):
                sc[i] = scores(*units[i])
            if 1 <= i <= len(units):
                sm[i - 1] = softmax(*sc.pop(i - 1))
            if i >= 2:
                finish(*units[i - 2], *sm.pop(i - 2))

    ng = len(DIL_PATTERNS)
    mx = ol[0]
    for g in range(1, ng):
        mx = jnp.maximum(mx, ol[g])
    num = jnp.zeros((s, HEAD), F32)
    den = jnp.zeros((s, HEAD), F32)
    for g in range(ng):
        e = jnp.exp(ol[g] - mx)
        num = num + e * og[g]
        den = den + e
    o_ref[0] = (num / den).astype(o_ref.dtype)


def _dilated(proj, rot_cos, rot_sin, col0):
    b, s, _ = proj.shape
    assert s % (DIL_BLOCK * max(r for _, r in DIL_PATTERNS)) == 0
    hg = DIL_HEADS_PER_GROUP
    ng = len(DIL_PATTERNS)
    width = ng * hg
    base = col0 // HEAD
    specs = []
    for g in range(ng):
        for part in range(3):
            off = base + part * width + g * hg
            specs.append(pl.BlockSpec((1, s, HEAD), lambda i, j, off=off: (i, 0, off + j)))
    tab = pl.BlockSpec((1, s, HEAD), lambda i, j: (i, 0, 0))
    return pl.pallas_call(
        _dilated_kernel,
        grid=(b, hg),
        in_specs=specs + [tab, tab],
        out_specs=pl.BlockSpec((1, s, HEAD), lambda i, j: (i, 0, j)),
        out_shape=jax.ShapeDtypeStruct((b, s, hg * HEAD), BF16),
        scratch_shapes=[pltpu.VMEM((s, HEAD), F32)] * 3
        + [pltpu.VMEM((ng, s, HEAD), F32)] * 2
        + [pltpu.VMEM((2 * len(_DIL_WINDOWS), DIL_BLOCK, DIL_BLOCK), F32)],
        compiler_params=_params("parallel", "parallel"),
        name="dilated",
    )(*([proj] * 9), rot_cos, rot_sin)


def _merge_kernel(ret_ref, dil_ref, wr_ref, wd_ref, gr_ref, gd_ref, o_ref):
    a = jnp.dot(ret_ref[...], wr_ref[...], preferred_element_type=F32)
    d = jnp.dot(dil_ref[...], wd_ref[...], preferred_element_type=F32)
    gr = jax.nn.sigmoid(gr_ref[...].astype(F32))
    gd = jax.nn.sigmoid(gd_ref[...].astype(F32))
    o_ref[...] = (gr * a + gd * d).astype(o_ref.dtype)


def _merge(ret, dil, w_br_ret, w_br_dil, proj2d, gate_col0, *, tm, tn):
    m, d_model = ret.shape[0], w_br_ret.shape[1]
    assert gate_col0 % tn == 0 and d_model % tn == 0 and m % tm == 0
    gr0 = gate_col0 // tn
    gd0 = gr0 + d_model // tn
    return pl.pallas_call(
        _merge_kernel,
        grid=(m // tm, d_model // tn),
        in_specs=[
            pl.BlockSpec((tm, ret.shape[1]), lambda i, j: (i, 0)),
            pl.BlockSpec((tm, dil.shape[1]), lambda i, j: (i, 0)),
            pl.BlockSpec((w_br_ret.shape[0], tn), lambda i, j: (0, j)),
            pl.BlockSpec((w_br_dil.shape[0], tn), lambda i, j: (0, j)),
            pl.BlockSpec((tm, tn), lambda i, j: (i, gr0 + j)),
            pl.BlockSpec((tm, tn), lambda i, j: (i, gd0 + j)),
        ],
        out_specs=pl.BlockSpec((tm, tn), lambda i, j: (i, j)),
        out_shape=jax.ShapeDtypeStruct((m, d_model), BF16),
        compiler_params=_params("parallel", "parallel"),
        name="merge",
    )(ret, dil, w_br_ret, w_br_dil, proj2d, proj2d)


def _mem_attn_kernel(q_ref, k_ref, v_ref, o_ref):
    dh = q_ref.shape[2]
    sc = lax.dot_general(q_ref[0], k_ref[0], NT_DIMS, preferred_element_type=F32) * (dh ** -0.5)
    m = jnp.max(sc, axis=-1, keepdims=True)
    p = jnp.exp(sc - m)
    l = jnp.sum(p, axis=-1, keepdims=True)
    o = jnp.dot(p.astype(BF16), v_ref[0], preferred_element_type=F32)
    o_ref[0] = (o / l).astype(o_ref.dtype)


def _mem_attention(q, kv, *, ts):
    b, s, d = q.shape
    m = kv.shape[1]
    dh = d // MEM_HEADS
    return pl.pallas_call(
        _mem_attn_kernel,
        grid=(b, s // ts, MEM_HEADS),
        in_specs=[
            pl.BlockSpec((1, ts, dh), lambda i, j, h: (i, j, h)),
            pl.BlockSpec((1, m, dh), lambda i, j, h: (i, 0, h)),
            pl.BlockSpec((1, m, dh), lambda i, j, h: (i, 0, MEM_HEADS + h)),
        ],
        out_specs=pl.BlockSpec((1, ts, dh), lambda i, j, h: (i, j, h)),
        out_shape=jax.ShapeDtypeStruct((b, s, d), BF16),
        compiler_params=_params("parallel", "parallel", "parallel"),
        name="mem_attention",
    )(q, kv, kv)


NOT_RANKED = 127.0
TOPK_COLUMNS_PER_ITER = 4


def _extract_top(ws, count, tie_break, want_rank=True):
    ws = list(ws)
    n = ws[0].shape[0]
    idx = lax.broadcasted_iota(jnp.int32, ws[0].shape, 0).astype(F32)
    wants = list(want_rank) if isinstance(want_rank, (list, tuple)) else [want_rank] * len(ws)
    ranks = [jnp.full(w.shape, NOT_RANKED, F32) if want else None for w, want in zip(ws, wants)]
    vals = [[] for _ in ws]
    for it in range(count):
        for p, w in enumerate(ws):
            m = jnp.max(w, axis=0, keepdims=True)
            if tie_break:
                first = jnp.min(jnp.where(w == m, idx, float(n)), axis=0, keepdims=True)
                hit = idx == first
            else:
                hit = w == m
            ws[p] = jnp.where(hit, -jnp.inf, w)
            if wants[p]:
                ranks[p] = jnp.where(hit, float(it), ranks[p])
            vals[p].append(m)
    return list(zip(vals, ranks, ws))


_PAIR_CANDIDATES = tuple((a, b) for a in range(PEER_TOPK) for b in range(PEER_TOPK)
                         if (a + 1) * (b + 1) <= PEER_TOPK)


def _peer_topk_kernel(q_ref, sk_ref, r2_ref, e2_ref, n1_ref, e1_ref):
    k = PEER_TOPK
    lanes = 128
    pad_rows = -len(_PAIR_CANDIDATES) % 8

    def columns(cs, tie_break):
        toks = [pl.ds(pl.multiple_of(c * lanes, lanes), lanes) for c in cs]
        sc1s = [lax.dot_general(sk_ref[0, 0], q_ref[tok, :HEAD], NT_DIMS,
                                preferred_element_type=F32) for tok in toks]
        sc2s = [lax.dot_general(sk_ref[0, 1], q_ref[tok, HEAD:], NT_DIMS,
                                preferred_element_type=F32) for tok in toks]
        stage1 = _extract_top(sc1s + sc2s, k, tie_break,
                              want_rank=[tie_break] * len(cs) + [True] * len(cs))
        first, second = stage1[:len(cs)], stage1[len(cs):]
        combos = []
        for (v1, _, _), (v2, _, _) in zip(first, second):
            rows = [v1[a] + v2[b] for a, b in _PAIR_CANDIDATES]
            rows += [jnp.full_like(rows[0], -jnp.inf)] * pad_rows
            combos.append(jnp.concatenate(rows, axis=0))
        stage2 = _extract_top(combos, k, tie_break, want_rank=False)
        off_count = None
        for j, tok in enumerate(toks):
            off = finish(tok, sc1s[j], sc2s[j], first[j], second[j], combos[j], stage2[j])
            off_count = off if off_count is None else off_count + off
        return off_count

    def finish(tok, sc1, sc2, first, second, combo, pairs):
        v1, rank1, _ = first
        v2, rank2, _ = second
        cv, _, left = pairs
        taken = jnp.where(left < combo, 1.0, 0.0)
        z = jnp.zeros_like(cv[0])
        for i in range(k):
            z = z + jnp.exp(cv[i] - cv[0])
        n1 = jnp.zeros_like(sc1)
        for a in range(k):
            rows_a = [r for r, (ca, _) in enumerate(_PAIR_CANDIDATES) if ca == a]
            cnt = taken[rows_a[0]:rows_a[0] + 1, :]
            for r in rows_a[1:]:
                cnt = cnt + taken[r:r + 1, :]
            is_a = (sc1 == v1[a]) if rank1 is None else (rank1 == float(a))
            n1 = jnp.where(is_a, cnt, n1)
        sel1 = (sc1 >= v1[k - 1]) if rank1 is None else (rank1 < NOT_RANKED)
        sel2 = rank2 < NOT_RANKED
        r2_ref[0, :, tok] = rank2.astype(BF16)
        e2_ref[0, :, tok] = jnp.where(sel2, jnp.exp(sc2 - v2[0]), 0.0).astype(BF16)
        n1_ref[0, :, tok] = n1
        e1_ref[0, :, tok] = jnp.where(sel1, jnp.exp(sc1 - v1[0]), 0.0) / z
        count = lambda flags: jnp.sum(flags, axis=0, keepdims=True)
        return (jnp.abs(count(jnp.where(sel1, 1.0, 0.0)) - k) + jnp.abs(count(jnp.where(sel2, 1.0, 0.0)) - k)
                + jnp.abs(count(taken) - k))

    def group(g, _):
        cols = [g * TOPK_COLUMNS_PER_ITER + off for off in range(TOPK_COLUMNS_PER_ITER)]
        off_count = columns(cols, tie_break=False)

        @pl.when(jnp.max(off_count) > 0.0)
        def _():
            columns(cols, tie_break=True)

        return 0

    lax.fori_loop(0, q_ref.shape[0] // (lanes * TOPK_COLUMNS_PER_ITER), group, 0)


def _peer_topk(qp, subkeys, *, tk):
    t = qp.shape[0]
    h, _, nk, dq = subkeys.shape
    spec = pl.BlockSpec((1, nk, tk), lambda i, j: (j, 0, i))
    narrow = jax.ShapeDtypeStruct((h, nk, t), BF16)
    wide = jax.ShapeDtypeStruct((h, nk, t), F32)
    return pl.pallas_call(
        _peer_topk_kernel,
        grid=(t // tk, h),
        in_specs=[pl.BlockSpec((tk, 2 * dq), lambda i, j: (i, j)),
                  pl.BlockSpec((1, 2, nk, dq), lambda i, j: (j, 0, 0, 0))],
        out_specs=[spec] * 4,
        out_shape=[narrow, narrow, wide, wide],
        compiler_params=_params("parallel", "parallel"),
        name="peer_topk",
    )(qp, subkeys)


BF16_SUBLANES = 16


def _row_to_bf16_tile(row, rows):
    tile = jnp.broadcast_to(row, (BF16_SUBLANES, row.shape[1])).astype(BF16)
    return jnp.concatenate([tile] * (rows // BF16_SUBLANES), axis=0)


GATE_ROUNDTRIPS = 3


def _lane_roundtrip(x):
    rows, cols = x.shape
    words = pltpu.bitcast(x, jnp.uint32)
    parts = []
    for c in range(cols // 128):
        part = words[:, c * 128:(c + 1) * 128]
        for _ in range(GATE_ROUNDTRIPS):
            part = pltpu.roll(pltpu.roll(part, 1, 1), 127, 1)
        parts.append(part)
    return pltpu.bitcast(jnp.concatenate(parts, axis=1), x.dtype)


def _peer_ffn_kernel(xn_ref, u_ref, v_ref, r2_ref, e2_ref, n1_ref, e1_ref, res_ref, gf_ref, o_ref,
                     w_ref, *, final_norm):
    j = pl.program_id(1)
    ne = u_ref.shape[0]
    nk = PEER_N_KEYS

    @pl.when(j == 0)
    def _():
        o_ref[...] = jnp.zeros_like(o_ref)

    a = lax.dot_general(u_ref[...], xn_ref[...], NT_DIMS, preferred_element_type=F32)
    for b in range(ne // nk):
        gate = None
        for h in range(PEER_HEADS):
            n1 = _row_to_bf16_tile(n1_ref[h, b:b + 1, :], nk)
            e1 = _row_to_bf16_tile(e1_ref[h, b:b + 1, :], nk)
            term = jnp.where(r2_ref[h] < n1, e2_ref[h], jnp.zeros((), BF16)) * e1
            gate = term if gate is None else gate + term
        gate = _lane_roundtrip(gate)
        ab = a[b * nk:(b + 1) * nk, :]
        act = 0.5 * ab * (1.0 + lax.erf(ab * (0.5 ** 0.5)))
        w_ref[b * nk:(b + 1) * nk, :] = act.astype(BF16) * gate
    o_ref[...] += lax.dot_general(w_ref[...], v_ref[...], TN_DIMS, preferred_element_type=F32)

    @pl.when(j == pl.num_programs(1) - 1)
    def _():
        y = res_ref[...] + o_ref[...]
        if final_norm:
            ms = jnp.mean(y * y, axis=-1, keepdims=True)
            y = (y * lax.rsqrt(ms + NORM_EPS)) * gf_ref[...]
        o_ref[...] = y


def _peer_ffn(xn, u, v, r2, e2, n1, e1, res, g_final, *, tt, ne, final_norm):
    t, d = xn.shape
    n_exp = u.shape[0]
    h, nk, _ = r2.shape
    rows = ne // nk
    assert n_exp == nk * nk and ne % nk == 0 and t % tt == 0 and n_exp % ne == 0 and rows % 8 == 0
    keys = pl.BlockSpec((h, nk, tt), lambda i, j: (0, 0, i))
    first = pl.BlockSpec((h, rows, tt), lambda i, j: (0, j, i))
    return pl.pallas_call(
        functools.partial(_peer_ffn_kernel, final_norm=final_norm),
        grid=(t // tt, n_exp // ne),
        in_specs=[
            pl.BlockSpec((tt, d), lambda i, j: (i, 0)),
            pl.BlockSpec((ne, d), lambda i, j: (j, 0)),
            pl.BlockSpec((ne, d), lambda i, j: (j, 0)),
            keys, keys, first, first,
            pl.BlockSpec((tt, d), lambda i, j: (i, 0)),
            pl.BlockSpec((1, d), lambda i, j: (0, 0)),
        ],
        out_specs=pl.BlockSpec((tt, d), lambda i, j: (i, 0)),
        out_shape=jax.ShapeDtypeStruct((t, d), F32),
        scratch_shapes=[pltpu.VMEM((ne, tt), BF16)],
        compiler_params=_params("parallel", "arbitrary"),
        name="peer_ffn",
    )(xn, u, v, r2, e2, n1, e1, res, g_final.reshape(1, d).astype(F32))


def kernel(x, mem, positions, g_mix, w_in, w_br_ret, w_br_dil, w_out, g_cross, g_mem, w_q_mem,
           w_kv_mem, w_o_mem, g_ffn, w_peer_q, peer_subkeys, peer_u, peer_v, g_final):
    b, s, d = x.shape
    t = b * s
    depth = w_in.shape[0]
    ret_w = RET_HEADS * HEAD
    dil_col0 = 4 * ret_w
    dil_w = len(DIL_PATTERNS) * DIL_HEADS_PER_GROUP * HEAD
    gate_col0 = dil_col0 + 3 * dil_w
    mem2d = mem.reshape(-1, d)
    tl = _tiles(t, s, mem2d.shape[0])

    rot_cos, rot_sin, ret_cos, ret_sin = _rotation_tables(positions)
    x2d = x.reshape(t, d)
    for l in range(depth):
        bf = lambda w: w[l].astype(BF16)
        proj = _norm_matmul(x2d, g_mix[l], bf(w_in), tm=tl.rows, tn=tl.proj_cols, out_dtype=BF16)
        proj3 = proj.reshape(b, s, -1)
        ret = _retention(proj3, ret_cos, ret_sin).reshape(t, ret_w)
        dil = _dilated(proj3, rot_cos, rot_sin, dil_col0).reshape(t, -1)
        merged = _merge(ret, dil, bf(w_br_ret), bf(w_br_dil), proj, gate_col0, tm=tl.rows,
                        tn=tl.merge_cols)
        x2d = _matmul_res(merged, bf(w_out), x2d, tm=tl.rows, tn=tl.cols)
        kv = _norm_matmul(mem2d, g_mem[l], bf(w_kv_mem), tm=tl.mem_rows, tn=tl.cols, out_dtype=BF16)
        qm = _norm_matmul(x2d, g_cross[l], bf(w_q_mem), tm=tl.rows, tn=tl.cols, out_dtype=BF16)
        att = _mem_attention(qm.reshape(b, s, d), kv.reshape(b, -1, 2 * d), ts=tl.attn_rows)
        x2d = _matmul_res(att.reshape(t, d), bf(w_o_mem), x2d, tm=tl.rows, tn=tl.cols)
        qp, xn = _norm_matmul(x2d, g_ffn[l], bf(w_peer_q), tm=tl.rows, tn=tl.cols, out_dtype=BF16,
                              emit_hn=True)
        r2, e2, n1, e1 = _peer_topk(qp, bf(peer_subkeys), tk=tl.topk_tokens)
        x2d = _peer_ffn(xn, bf(peer_u), bf(peer_v), r2, e2, n1, e1, x2d, g_final,
                        tt=tl.ffn_tokens, ne=tl.ffn_experts, final_norm=(l == depth - 1))
    return x2d.reshape(b, s, d)
```

```python
import functools
from typing import NamedTuple

import jax
import jax.numpy as jnp
from jax import lax
from jax.experimental import pallas as pl
from jax.experimental.pallas import tpu as pltpu

F32 = jnp.float32
BF16 = jnp.bfloat16

NORM_EPS = 1e-6
ROPE_THETA = 10000.0
HEAD = 128
RET_HEADS = 8
RET_CHUNK = 128
DIL_PATTERNS = ((128, 1), (512, 4), (2048, 16))
DIL_HEADS_PER_GROUP = 4
DIL_BLOCK = 128
MEM_HEADS = 4
PEER_HEADS = 8
PEER_N_KEYS = 128
PEER_TOPK = 16

V7X_VMEM_BYTES = 64 * 1024 * 1024
COMPILER_RESERVE_BYTES = 8 * 1024 * 1024
VMEM_LIMIT = V7X_VMEM_BYTES - COMPILER_RESERVE_BYTES

NT_DIMS = (((1,), (1,)), ((), ()))
TN_DIMS = (((0,), (0,)), ((), ()))


class Tiles(NamedTuple):
    rows: int
    mem_rows: int
    proj_cols: int
    norm_cols: int
    cols: int
    merge_rows: int
    merge_cols: int
    attn_rows: int
    topk_tokens: int
    ffn_tokens: int
    ffn_experts: int


def _tiles(t, s, mem_rows):
    return Tiles(rows=min(1024, t), mem_rows=min(1024, mem_rows), proj_cols=2560, norm_cols=2048,
                 cols=1024, merge_rows=min(2048, t), merge_cols=512, attn_rows=min(2048, s),
                 topk_tokens=min(1024, t), ffn_tokens=512, ffn_experts=1024)


def _params(*sem):
    return pltpu.CompilerParams(dimension_semantics=sem, vmem_limit_bytes=VMEM_LIMIT)


def _norm_rows(x_ref, g_ref, hn_ref, rows):
    tm = x_ref.shape[0]

    def body(c, _):
        sl = pl.ds(pl.multiple_of(c * rows, rows), rows)
        x = x_ref[sl, :].astype(F32)
        ms = jnp.mean(x * x, axis=-1, keepdims=True)
        hn_ref[sl, :] = ((x * lax.rsqrt(ms + NORM_EPS)) * g_ref[...]).astype(BF16)
        return 0

    lax.fori_loop(0, tm // rows, body, 0)


def _norm_matmul_kernel(x_ref, g_ref, w_ref, o_ref, hn_ref):
    @pl.when(pl.program_id(1) == 0)
    def _():
        _norm_rows(x_ref, g_ref, hn_ref, 256)

    o_ref[...] = jnp.dot(hn_ref[...], w_ref[...], preferred_element_type=F32).astype(o_ref.dtype)


def _norm_matmul(x, g, w, *, tm, tn, out_dtype, emit_hn=False):
    m, k = x.shape
    n = w.shape[1]
    assert m % tm == 0 and n % tn == 0
    in_specs = [
        pl.BlockSpec((tm, k), lambda i, j: (i, 0)),
        pl.BlockSpec((1, k), lambda i, j: (0, 0)),
        pl.BlockSpec((k, tn), lambda i, j: (0, j)),
    ]
    o_spec = pl.BlockSpec((tm, tn), lambda i, j: (i, j))
    o_shape = jax.ShapeDtypeStruct((m, n), out_dtype)
    if emit_hn:
        out_specs = [o_spec, pl.BlockSpec((tm, k), lambda i, j: (i, 0))]
        out_shape = [o_shape, jax.ShapeDtypeStruct((m, k), BF16)]
        scratch = []
    else:
        out_specs = o_spec
        out_shape = o_shape
        scratch = [pltpu.VMEM((tm, k), BF16)]
    return pl.pallas_call(
        _norm_matmul_kernel,
        grid=(m // tm, n // tn),
        in_specs=in_specs,
        out_specs=out_specs,
        out_shape=out_shape,
        scratch_shapes=scratch,
        compiler_params=_params("parallel", "arbitrary"),
        name="norm_matmul",
    )(x, g.reshape(1, k).astype(F32), w)


def _matmul_res_kernel(a_ref, w_ref, r_ref, o_ref):
    o_ref[...] = r_ref[...] + jnp.dot(a_ref[...], w_ref[...], preferred_element_type=F32)


def _matmul_res(a, w, res, *, tm, tn):
    m, k = a.shape
    n = w.shape[1]
    assert m % tm == 0 and n % tn == 0
    return pl.pallas_call(
        _matmul_res_kernel,
        grid=(m // tm, n // tn),
        in_specs=[
            pl.BlockSpec((tm, k), lambda i, j: (i, 0)),
            pl.BlockSpec((k, tn), lambda i, j: (0, j)),
            pl.BlockSpec((tm, tn), lambda i, j: (i, j)),
        ],
        out_specs=pl.BlockSpec((tm, tn), lambda i, j: (i, j)),
        out_shape=jax.ShapeDtypeStruct((m, n), F32),
        compiler_params=_params("parallel", "parallel"),
        name="matmul_res",
    )(a, w, res)


def _tables_kernel(pos_ref, inv_ref, rc_ref, rs_ref, tc_ref, ts_ref, *, rows):
    s = pos_ref.shape[1]
    lane = lax.broadcasted_iota(jnp.int32, (1, HEAD), 1)
    half_sign = jnp.where(lane < HEAD // 2, -1.0, 1.0).astype(F32)
    pair_sign = jnp.where(lane % 2 == 0, -1.0, 1.0).astype(F32)

    def body(c, _):
        sl = pl.ds(pl.multiple_of(c * rows, rows), rows)
        pos = pos_ref[0, sl, :]
        ang = pos * inv_ref[0:1, :]
        rc_ref[0, sl, :] = jnp.cos(ang)
        rs_ref[0, sl, :] = jnp.sin(ang) * half_sign
        ang = pos * inv_ref[1:2, :]
        tc_ref[0, sl, :] = jnp.cos(ang)
        ts_ref[0, sl, :] = jnp.sin(ang) * pair_sign
        return 0

    lax.fori_loop(0, s // rows, body, 0)


def _rotation_tables(positions):
    b, s = positions.shape
    half = HEAD // 2
    inv_half = 1.0 / (ROPE_THETA ** (jnp.arange(half, dtype=F32) / half))
    inv_pair = 1.0 / (10000.0 ** jnp.linspace(0.0, 1.0, half, dtype=F32))
    inv = jnp.stack([jnp.concatenate([inv_half, inv_half]), jnp.repeat(inv_pair, 2)])
    pos = positions.astype(F32).reshape(b, s, 1)
    tab = jax.ShapeDtypeStruct((b, s, HEAD), F32)
    spec = pl.BlockSpec((1, s, HEAD), lambda i: (i, 0, 0))
    return pl.pallas_call(
        functools.partial(_tables_kernel, rows=256),
        grid=(b,),
        in_specs=[pl.BlockSpec((1, s, 1), lambda i: (i, 0, 0)),
                  pl.BlockSpec((2, HEAD), lambda i: (0, 0))],
        out_specs=[spec] * 4,
        out_shape=[tab] * 4,
        compiler_params=_params("parallel"),
        name="rotation_tables",
    )(pos, inv)


RET_HEADS_PER_STEP = 8


def _retention_kernel(logg_ref, q_ref, k_ref, v_ref, g_ref, cos_ref, sin_ref, o_ref,
                      state_ref, dmat_ref, kw_ref, qw_ref, decay_ref):
    c = RET_CHUNK
    s = q_ref.shape[1]
    hp = RET_HEADS_PER_STEP
    row = lax.broadcasted_iota(jnp.int32, (c, c), 0).astype(F32)
    col = lax.broadcasted_iota(jnp.int32, (c, c), 1).astype(F32)
    diff = row - col
    for h in range(hp):
        lg = logg_ref[pl.program_id(1) * hp + h]
        dmat_ref[h] = jnp.where(diff >= 0, jnp.exp(lg * jnp.maximum(diff, 0.0)), 0.0)
        kw_ref[h] = jnp.exp(lg * (c - 1.0 - row))
        qw_ref[h] = jnp.exp(lg * (row + 1.0))
        decay_ref[h] = jnp.exp(lg * jnp.full((c, c), float(c), F32))
    even = lax.broadcasted_iota(jnp.int32, (c, HEAD), 1) % 2 == 0
    scale = HEAD ** -0.5

    state_ref[...] = jnp.zeros_like(state_ref)

    def body(n, _):
        sl = pl.ds(pl.multiple_of(n * c, c), c)
        cos = cos_ref[0, sl, :]
        sin = sin_ref[0, sl, :]

        def rot(x):
            swapped = jnp.where(even, pltpu.roll(x, HEAD - 1, 1), pltpu.roll(x, 1, 1))
            return x * cos + swapped * sin

        heads = range(hp)
        cols = [slice(h * HEAD, (h + 1) * HEAD) for h in heads]
        q = [rot(q_ref[0, sl, cols[h]].astype(F32)) for h in heads]
        k = [rot(k_ref[0, sl, cols[h]].astype(F32)) * scale for h in heads]
        v = [v_ref[0, sl, cols[h]] for h in heads]
        state = [state_ref[h] for h in heads]
        scores = [lax.dot_general(q[h].astype(BF16), k[h].astype(BF16), NT_DIMS,
                                  preferred_element_type=F32) * dmat_ref[h] for h in heads]
        cross = [jnp.dot((q[h] * qw_ref[h]).astype(BF16), state[h].astype(BF16),
                         preferred_element_type=F32) for h in heads]
        kv = [lax.dot_general((k[h] * kw_ref[h]).astype(BF16), v[h], TN_DIMS,
                              preferred_element_type=F32) for h in heads]
        inner = [jnp.dot(scores[h].astype(BF16), v[h], preferred_element_type=F32) for h in heads]
        for h in heads:
            state_ref[h] = decay_ref[h] * state[h] + kv[h]
            out = inner[h] + cross[h]
            out = out * lax.rsqrt(jnp.mean(out * out, axis=-1, keepdims=True) + NORM_EPS)
            g = g_ref[0, sl, cols[h]].astype(F32)
            o_ref[0, sl, cols[h]] = (out * (g * jax.nn.sigmoid(g))).astype(o_ref.dtype)
        return 0

    lax.fori_loop(0, s // c, body, 0)


def _retention(proj, ret_cos, ret_sin):
    b, s, _ = proj.shape
    h = RET_HEADS
    hp = RET_HEADS_PER_STEP
    groups = h // hp
    log_g = jnp.log1p(-jnp.exp2(-5.0 - jnp.arange(h, dtype=F32)))
    heads = lambda part: pl.BlockSpec((1, s, hp * HEAD), lambda i, j: (i, 0, part * groups + j))
    tab = pl.BlockSpec((1, s, HEAD), lambda i, j: (i, 0, 0))
    per_head = pltpu.VMEM((hp, HEAD, HEAD), F32)
    return pl.pallas_call(
        _retention_kernel,
        grid=(b, groups),
        in_specs=[pl.BlockSpec(memory_space=pltpu.SMEM),
                  heads(0), heads(1), heads(2), heads(3), tab, tab],
        out_specs=pl.BlockSpec((1, s, hp * HEAD), lambda i, j: (i, 0, j)),
        out_shape=jax.ShapeDtypeStruct((b, s, h * HEAD), BF16),
        scratch_shapes=[per_head] * 5,
        compiler_params=_params("parallel", "parallel"),
        name="retention",
    )(log_g, proj, proj, proj, proj, ret_cos, ret_sin)


_DIL_WINDOWS = tuple(sorted({w // r for w, r in DIL_PATTERNS}))


def _dilated_kernel(*refs):
    qkv_refs = refs[:9]
    cos_ref, sin_ref, o_ref, qs, ks, vs, og, ol, bias_ref = refs[9:]
    s = o_ref.shape[1]
    blk = DIL_BLOCK
    scale = HEAD ** -0.5
    neg = -1e30
    qi = lax.broadcasted_iota(jnp.int32, (blk, blk), 0)
    kj = lax.broadcasted_iota(jnp.int32, (blk, blk), 1)
    cos = cos_ref[0]
    sin = sin_ref[0]

    def rot(x):
        return x * cos + pltpu.roll(x, HEAD // 2, 1) * sin

    for wi, win in enumerate(_DIL_WINDOWS):
        cur_ok = (qi - kj >= 0) & (qi - kj <= win)
        prev_ok = (qi - kj + blk) <= win
        bias_ref[2 * wi] = jnp.where(cur_ok, 0.0, neg)
        bias_ref[2 * wi + 1] = jnp.where(prev_ok, 0.0, neg)

    for g, (window, r) in enumerate(DIL_PATTERNS):
        wi = _DIL_WINDOWS.index(window // r)
        length = s // r
        nb = length // blk
        q_ref, k_ref, v_ref = qkv_refs[3 * g:3 * g + 3]
        qs[...] = rot(q_ref[0].astype(F32)) * scale
        ks[...] = rot(k_ref[0].astype(F32))
        vs[...] = v_ref[0].astype(F32)
        units = [(c, n) for c in range(r) for n in range(nb)]

        def rows(c, m):
            return pl.ds(c + m * blk * r, blk, stride=r) if r > 1 else pl.ds(m * blk, blk)

        def scores(c, n):
            q = qs[rows(c, n), :].astype(BF16)
            k1 = ks[rows(c, n), :].astype(BF16)
            s1 = lax.dot_general(q, k1, NT_DIMS, preferred_element_type=F32) + bias_ref[2 * wi]
            s0 = None
            if n > 0:
                k0 = ks[rows(c, n - 1), :].astype(BF16)
                s0 = (lax.dot_general(q, k0, NT_DIMS, preferred_element_type=F32)
                      + bias_ref[2 * wi + 1])
            return s1, s0

        def softmax(s1, s0):
            m = jnp.max(s1, axis=-1, keepdims=True)
            if s0 is not None:
                m = jnp.maximum(m, jnp.max(s0, axis=-1, keepdims=True))
            p1 = jnp.exp(s1 - m)
            l = jnp.sum(p1, axis=-1, keepdims=True)
            p0 = None
            if s0 is not None:
                p0 = jnp.exp(s0 - m)
                l = l + jnp.sum(p0, axis=-1, keepdims=True)
                p0 = p0.astype(BF16)
            return p1.astype(BF16), p0, m, l

        def finish(c, n, p1, p0, m, l):
            acc = jnp.dot(p1, vs[rows(c, n), :].astype(BF16), preferred_element_type=F32)
            if p0 is not None:
                acc = acc + jnp.dot(p0, vs[rows(c, n - 1), :].astype(BF16),
                                    preferred_element_type=F32)
            og[g, rows(c, n), :] = acc / l
            ol[g, rows(c, n), :] = jnp.broadcast_to(m + jnp.log(l), (blk, HEAD))

        sc, sm = {}, {}
        for i in range(len(units) + 2):
            if i < len(units):
                sc[i] = scores(*units[i])
            if 1 <= i <= len(units):
                sm[i - 1] = softmax(*sc.pop(i - 1))
            if i >= 2:
                finish(*units[i - 2], *sm.pop(i - 2))

    ng = len(DIL_PATTERNS)
    mx = ol[0]
    for g in range(1, ng):
        mx = jnp.maximum(mx, ol[g])
    num = jnp.zeros((s, HEAD), F32)
    den = jnp.zeros((s, HEAD), F32)
    for g in range(ng):
        e = jnp.exp(ol[g] - mx)
        num = num + e * og[g]
        den = den + e
    o_ref[0] = (num / den).astype(o_ref.dtype)


def _dilated(proj, rot_cos, rot_sin, col0):
    b, s, _ = proj.shape
    assert s % (DIL_BLOCK * max(r for _, r in DIL_PATTERNS)) == 0
    hg = DIL_HEADS_PER_GROUP
    ng = len(DIL_PATTERNS)
    width = ng * hg
    base = col0 // HEAD
    specs = []
    for g in range(ng):
        for part in range(3):
            off = base + part * width + g * hg
            specs.append(pl.BlockSpec((1, s, HEAD), lambda i, j, off=off: (i, 0, off + j)))
    tab = pl.BlockSpec((1, s, HEAD), lambda i, j: (i, 0, 0))
    return pl.pallas_call(
        _dilated_kernel,
        grid=(b, hg),
        in_specs=specs + [tab, tab],
        out_specs=pl.BlockSpec((1, s, HEAD), lambda i, j: (i, 0, j)),
        out_shape=jax.ShapeDtypeStruct((b, s, hg * HEAD), BF16),
        scratch_shapes=[pltpu.VMEM((s, HEAD), F32)] * 3
        + [pltpu.VMEM((ng, s, HEAD), F32)] * 2
        + [pltpu.VMEM((2 * len(_DIL_WINDOWS), DIL_BLOCK, DIL_BLOCK), F32)],
        compiler_params=_params("parallel", "parallel"),
        name="dilated",
    )(*([proj] * 9), rot_cos, rot_sin)


def _merge_kernel(ret_ref, dil_ref, wr_ref, wd_ref, gr_ref, gd_ref, o_ref):
    a = jnp.dot(ret_ref[...], wr_ref[...], preferred_element_type=F32)
    d = jnp.dot(dil_ref[...], wd_ref[...], preferred_element_type=F32)
    gr = jax.nn.sigmoid(gr_ref[...].astype(F32))
    gd = jax.nn.sigmoid(gd_ref[...].astype(F32))
    o_ref[...] = (gr * a + gd * d).astype(o_ref.dtype)


def _merge(ret, dil, w_br_ret, w_br_dil, proj2d, gate_col0, *, tm, tn):
    m, d_model = ret.shape[0], w_br_ret.shape[1]
    assert gate_col0 % tn == 0 and d_model % tn == 0 and m % tm == 0
    gr0 = gate_col0 // tn
    gd0 = gr0 + d_model // tn
    return pl.pallas_call(
        _merge_kernel,
        grid=(m // tm, d_model // tn),
        in_specs=[
            pl.BlockSpec((tm, ret.shape[1]), lambda i, j: (i, 0)),
            pl.BlockSpec((tm, dil.shape[1]), lambda i, j: (i, 0)),
            pl.BlockSpec((w_br_ret.shape[0], tn), lambda i, j: (0, j)),
            pl.BlockSpec((w_br_dil.shape[0], tn), lambda i, j: (0, j)),
            pl.BlockSpec((tm, tn), lambda i, j: (i, gr0 + j)),
            pl.BlockSpec((tm, tn), lambda i, j: (i, gd0 + j)),
        ],
        out_specs=pl.BlockSpec((tm, tn), lambda i, j: (i, j)),
        out_shape=jax.ShapeDtypeStruct((m, d_model), BF16),
        compiler_params=_params("parallel", "parallel"),
        name="merge",
    )(ret, dil, w_br_ret, w_br_dil, proj2d, proj2d)


def _mem_attn_kernel(q_ref, k_ref, v_ref, o_ref):
    dh = q_ref.shape[2]
    sc = lax.dot_general(q_ref[0], k_ref[0], NT_DIMS, preferred_element_type=F32) * (dh ** -0.5)
    m = jnp.max(sc, axis=-1, keepdims=True)
    p = jnp.exp(sc - m)
    l = jnp.sum(p, axis=-1, keepdims=True)
    o = jnp.dot(p.astype(BF16), v_ref[0], preferred_element_type=F32)
    o_ref[0] = (o / l).astype(o_ref.dtype)


def _mem_attention(q, kv, *, ts):
    b, s, d = q.shape
    m = kv.shape[1]
    dh = d // MEM_HEADS
    return pl.pallas_call(
        _mem_attn_kernel,
        grid=(b, s // ts, MEM_HEADS),
        in_specs=[
            pl.BlockSpec((1, ts, dh), lambda i, j, h: (i, j, h)),
            pl.BlockSpec((1, m, dh), lambda i, j, h: (i, 0, h)),
            pl.BlockSpec((1, m, dh), lambda i, j, h: (i, 0, MEM_HEADS + h)),
        ],
        out_specs=pl.BlockSpec((1, ts, dh), lambda i, j, h: (i, j, h)),
        out_shape=jax.ShapeDtypeStruct((b, s, d), BF16),
        compiler_params=_params("parallel", "parallel", "parallel"),
        name="mem_attention",
    )(q, kv, kv)


NOT_RANKED = 127.0
TOPK_COLUMNS_PER_ITER = 8


def _extract_top(ws, count, tie_break, want_rank=True):
    ws = list(ws)
    n = ws[0].shape[0]
    idx = lax.broadcasted_iota(jnp.int32, ws[0].shape, 0).astype(F32)
    wants = list(want_rank) if isinstance(want_rank, (list, tuple)) else [want_rank] * len(ws)
    ranks = [jnp.full(w.shape, NOT_RANKED, F32) if want else None for w, want in zip(ws, wants)]
    vals = [[] for _ in ws]
    for it in range(count):
        for p, w in enumerate(ws):
            m = jnp.max(w, axis=0, keepdims=True)
            if tie_break:
                first = jnp.min(jnp.where(w == m, idx, float(n)), axis=0, keepdims=True)
                hit = idx == first
            else:
                hit = w == m
            ws[p] = jnp.where(hit, -jnp.inf, w)
            if wants[p]:
                ranks[p] = jnp.where(hit, float(it), ranks[p])
            vals[p].append(m)
    return list(zip(vals, ranks, ws))


_PAIR_CANDIDATES = tuple((a, b) for a in range(PEER_TOPK) for b in range(PEER_TOPK)
                         if (a + 1) * (b + 1) <= PEER_TOPK)


def _peer_topk_kernel(q_ref, sk_ref, r2_ref, e2_ref, n1_ref, e1_ref):
    k = PEER_TOPK
    lanes = 128
    pad_rows = -len(_PAIR_CANDIDATES) % 8

    def columns(cs, tie_break):
        toks = [pl.ds(pl.multiple_of(c * lanes, lanes), lanes) for c in cs]
        sc1s = [lax.dot_general(sk_ref[0, 0], q_ref[tok, :HEAD], NT_DIMS,
                                preferred_element_type=F32) for tok in toks]
        sc2s = [lax.dot_general(sk_ref[0, 1], q_ref[tok, HEAD:], NT_DIMS,
                                preferred_element_type=F32) for tok in toks]
        stage1 = _extract_top(sc1s + sc2s, k, tie_break,
                              want_rank=[tie_break] * len(cs) + [True] * len(cs))
        first, second = stage1[:len(cs)], stage1[len(cs):]
        combos = []
        for (v1, _, _), (v2, _, _) in zip(first, second):
            rows = [v1[a] + v2[b] for a, b in _PAIR_CANDIDATES]
            rows += [jnp.full_like(rows[0], -jnp.inf)] * pad_rows
            combos.append(jnp.concatenate(rows, axis=0))
        stage2 = _extract_top(combos, k, tie_break, want_rank=False)
        off_count = None
        for j, tok in enumerate(toks):
            off = finish(tok, sc1s[j], sc2s[j], first[j], second[j], combos[j], stage2[j])
            off_count = off if off_count is None else off_count + off
        return off_count

    def finish(tok, sc1, sc2, first, second, combo, pairs):
        v1, rank1, _ = first
        v2, rank2, _ = second
        cv, _, left = pairs
        taken = jnp.where(left < combo, 1.0, 0.0)
        z = jnp.zeros_like(cv[0])
        for i in range(k):
            z = z + jnp.exp(cv[i] - cv[0])
        n1 = jnp.zeros_like(sc1)
        for a in range(k):
            rows_a = [r for r, (ca, _) in enumerate(_PAIR_CANDIDATES) if ca == a]
            cnt = taken[rows_a[0]:rows_a[0] + 1, :]
            for r in rows_a[1:]:
                cnt = cnt + taken[r:r + 1, :]
            is_a = (sc1 == v1[a]) if rank1 is None else (rank1 == float(a))
            n1 = jnp.where(is_a, cnt, n1)
        sel1 = (sc1 >= v1[k - 1]) if rank1 is None else (rank1 < NOT_RANKED)
        sel2 = rank2 < NOT_RANKED
        r2_ref[0, :, tok] = rank2.astype(BF16)
        e2_ref[0, :, tok] = jnp.where(sel2, jnp.exp(sc2 - v2[0]), 0.0).astype(BF16)
        n1_ref[0, :, tok] = n1
        e1_ref[0, :, tok] = jnp.where(sel1, jnp.exp(sc1 - v1[0]), 0.0) / z
        count = lambda flags: jnp.sum(flags, axis=0, keepdims=True)
        return (jnp.abs(count(jnp.where(sel1, 1.0, 0.0)) - k) + jnp.abs(count(jnp.where(sel2, 1.0, 0.0)) - k)
                + jnp.abs(count(taken) - k))

    def group(g, _):
        cols = [g * TOPK_COLUMNS_PER_ITER + off for off in range(TOPK_COLUMNS_PER_ITER)]
        off_count = columns(cols, tie_break=False)

        @pl.when(jnp.max(off_count) > 0.0)
        def _():
            columns(cols, tie_break=True)

        return 0

    lax.fori_loop(0, q_ref.shape[0] // (lanes * TOPK_COLUMNS_PER_ITER), group, 0)


def _peer_topk(qp, subkeys, *, tk):
    t = qp.shape[0]
    h, _, nk, dq = subkeys.shape
    assert t % tk == 0 and tk % (128 * TOPK_COLUMNS_PER_ITER) == 0
    spec = pl.BlockSpec((1, nk, tk), lambda i, j: (j, 0, i))
    narrow = jax.ShapeDtypeStruct((h, nk, t), BF16)
    wide = jax.ShapeDtypeStruct((h, nk, t), F32)
    return pl.pallas_call(
        _peer_topk_kernel,
        grid=(t // tk, h),
        in_specs=[pl.BlockSpec((tk, 2 * dq), lambda i, j: (i, j)),
                  pl.BlockSpec((1, 2, nk, dq), lambda i, j: (j, 0, 0, 0))],
        out_specs=[spec] * 4,
        out_shape=[narrow, narrow, wide, wide],
        compiler_params=_params("parallel", "parallel"),
        name="peer_topk",
    )(qp, subkeys)


BF16_SUBLANES = 16


def _row_to_bf16_tile(row, rows):
    tile = jnp.broadcast_to(row, (BF16_SUBLANES, row.shape[1])).astype(BF16)
    return jnp.concatenate([tile] * (rows // BF16_SUBLANES), axis=0)


GATE_ROUNDTRIPS = 3


def _lane_roundtrip(x):
    rows, cols = x.shape
    words = pltpu.bitcast(x, jnp.uint32)
    parts = []
    for c in range(cols // 128):
        part = words[:, c * 128:(c + 1) * 128]
        for _ in range(GATE_ROUNDTRIPS):
            part = pltpu.roll(pltpu.roll(part, 1, 1), 127, 1)
        parts.append(part)
    return pltpu.bitcast(jnp.concatenate(parts, axis=1), x.dtype)


def _peer_ffn_kernel(xn_ref, u_ref, v_ref, r2_ref, e2_ref, n1_ref, e1_ref, res_ref, gf_ref, o_ref,
                     w_ref, *, final_norm):
    j = pl.program_id(1)
    ne = u_ref.shape[0]
    nk = PEER_N_KEYS

    @pl.when(j == 0)
    def _():
        o_ref[...] = jnp.zeros_like(o_ref)

    a = lax.dot_general(u_ref[...], xn_ref[...], NT_DIMS, preferred_element_type=F32)
    for b in range(ne // nk):
        gate = None
        for h in range(PEER_HEADS):
            n1 = _row_to_bf16_tile(n1_ref[h, b:b + 1, :], nk)
            e1 = _row_to_bf16_tile(e1_ref[h, b:b + 1, :], nk)
            term = jnp.where(r2_ref[h] < n1, e2_ref[h], jnp.zeros((), BF16)) * e1
            gate = term if gate is None else gate + term
        gate = _lane_roundtrip(gate)
        ab = a[b * nk:(b + 1) * nk, :]
        act = 0.5 * ab * (1.0 + lax.erf(ab * (0.5 ** 0.5)))
        w_ref[b * nk:(b + 1) * nk, :] = act.astype(BF16) * gate
    o_ref[...] += lax.dot_general(w_ref[...], v_ref[...], TN_DIMS, preferred_element_type=F32)

    @pl.when(j == pl.num_programs(1) - 1)
    def _():
        y = res_ref[...] + o_ref[...]
        if final_norm:
            ms = jnp.mean(y * y, axis=-1, keepdims=True)
            y = (y * lax.rsqrt(ms + NORM_EPS)) * gf_ref[...]
        o_ref[...] = y


def _peer_ffn(xn, u, v, r2, e2, n1, e1, res, g_final, *, tt, ne, final_norm):
    t, d = xn.shape
    n_exp = u.shape[0]
    h, nk, _ = r2.shape
    rows = ne // nk
    assert n_exp == nk * nk and ne % nk == 0 and t % tt == 0 and n_exp % ne == 0 and rows % 8 == 0
    keys = pl.BlockSpec((h, nk, tt), lambda i, j: (0, 0, i))
    first = pl.BlockSpec((h, rows, tt), lambda i, j: (0, j, i))
    return pl.pallas_call(
        functools.partial(_peer_ffn_kernel, final_norm=final_norm),
        grid=(t // tt, n_exp // ne),
        in_specs=[
            pl.BlockSpec((tt, d), lambda i, j: (i, 0)),
            pl.BlockSpec((ne, d), lambda i, j: (j, 0)),
            pl.BlockSpec((ne, d), lambda i, j: (j, 0)),
            keys, keys, first, first,
            pl.BlockSpec((tt, d), lambda i, j: (i, 0)),
            pl.BlockSpec((1, d), lambda i, j: (0, 0)),
        ],
        out_specs=pl.BlockSpec((tt, d), lambda i, j: (i, 0)),
        out_shape=jax.ShapeDtypeStruct((t, d), F32),
        scratch_shapes=[pltpu.VMEM((ne, tt), BF16)],
        compiler_params=_params("parallel", "arbitrary"),
        name="peer_ffn",
    )(xn, u, v, r2, e2, n1, e1, res, g_final.reshape(1, d).astype(F32))


def kernel(x, mem, positions, g_mix, w_in, w_br_ret, w_br_dil, w_out, g_cross, g_mem, w_q_mem,
           w_kv_mem, w_o_mem, g_ffn, w_peer_q, peer_subkeys, peer_u, peer_v, g_final):
    b, s, d = x.shape
    t = b * s
    depth = w_in.shape[0]
    ret_w = RET_HEADS * HEAD
    dil_col0 = 4 * ret_w
    dil_w = len(DIL_PATTERNS) * DIL_HEADS_PER_GROUP * HEAD
    gate_col0 = dil_col0 + 3 * dil_w
    mem2d = mem.reshape(-1, d)
    tl = _tiles(t, s, mem2d.shape[0])

    rot_cos, rot_sin, ret_cos, ret_sin = _rotation_tables(positions)
    x2d = x.reshape(t, d)
    for l in range(depth):
        bf = lambda w: w[l].astype(BF16)
        proj = _norm_matmul(x2d, g_mix[l], bf(w_in), tm=tl.rows, tn=tl.proj_cols, out_dtype=BF16)
        proj3 = proj.reshape(b, s, -1)
        ret = _retention(proj3, ret_cos, ret_sin).reshape(t, ret_w)
        dil = _dilated(proj3, rot_cos, rot_sin, dil_col0).reshape(t, -1)
        merged = _merge(ret, dil, bf(w_br_ret), bf(w_br_dil), proj, gate_col0, tm=tl.merge_rows,
                        tn=tl.merge_cols)
        x2d = _matmul_res(merged, bf(w_out), x2d, tm=tl.rows, tn=tl.cols)
        kv = _norm_matmul(mem2d, g_mem[l], bf(w_kv_mem), tm=tl.mem_rows, tn=tl.norm_cols,
                          out_dtype=BF16)
        qm = _norm_matmul(x2d, g_cross[l], bf(w_q_mem), tm=tl.rows, tn=tl.norm_cols, out_dtype=BF16)
        att = _mem_attention(qm.reshape(b, s, d), kv.reshape(b, -1, 2 * d), ts=tl.attn_rows)
        x2d = _matmul_res(att.reshape(t, d), bf(w_o_mem), x2d, tm=tl.rows, tn=tl.cols)
        qp, xn = _norm_matmul(x2d, g_ffn[l], bf(w_peer_q), tm=tl.rows, tn=tl.norm_cols,
                              out_dtype=BF16, emit_hn=True)
        r2, e2, n1, e1 = _peer_topk(qp, bf(peer_subkeys), tk=tl.topk_tokens)
        x2d = _peer_ffn(xn, bf(peer_u), bf(peer_v), r2, e2, n1, e1, x2d, g_final,
                        tt=tl.ffn_tokens, ne=tl.ffn_experts, final_norm=(l == depth - 1))
    return x2d.reshape(b, s, d)
```

```python
import functools
from typing import NamedTuple

import jax
import jax.numpy as jnp
from jax import lax
from jax.experimental import pallas as pl
from jax.experimental.pallas import tpu as pltpu

F32 = jnp.float32
BF16 = jnp.bfloat16

NORM_EPS = 1e-6
ROPE_THETA = 10000.0
HEAD = 128
RET_HEADS = 8
RET_CHUNK = 128
DIL_PATTERNS = ((128, 1), (512, 4), (2048, 16))
DIL_HEADS_PER_GROUP = 4
DIL_BLOCK = 128
MEM_HEADS = 4
PEER_HEADS = 8
PEER_N_KEYS = 128
PEER_TOPK = 16

V7X_VMEM_BYTES = 64 * 1024 * 1024
COMPILER_RESERVE_BYTES = 8 * 1024 * 1024
VMEM_LIMIT = V7X_VMEM_BYTES - COMPILER_RESERVE_BYTES

NT_DIMS = (((1,), (1,)), ((), ()))
TN_DIMS = (((0,), (0,)), ((), ()))


class Tiles(NamedTuple):
    rows: int
    mem_rows: int
    proj_cols: int
    norm_cols: int
    cols: int
    merge_rows: int
    merge_cols: int
    attn_rows: int
    topk_tokens: int
    ffn_tokens: int
    ffn_experts: int


def _tiles(t, s, mem_rows):
    return Tiles(rows=min(1024, t), mem_rows=min(1024, mem_rows), proj_cols=2560, norm_cols=2048,
                 cols=1024, merge_rows=min(2048, t), merge_cols=512, attn_rows=min(2048, s),
                 topk_tokens=min(2048, t), ffn_tokens=512, ffn_experts=1024)


def _params(*sem):
    return pltpu.CompilerParams(dimension_semantics=sem, vmem_limit_bytes=VMEM_LIMIT)


def _norm_rows(x_ref, g_ref, hn_ref, rows):
    tm = x_ref.shape[0]

    def body(c, _):
        sl = pl.ds(pl.multiple_of(c * rows, rows), rows)
        x = x_ref[sl, :].astype(F32)
        ms = jnp.mean(x * x, axis=-1, keepdims=True)
        hn_ref[sl, :] = ((x * lax.rsqrt(ms + NORM_EPS)) * g_ref[...]).astype(BF16)
        return 0

    lax.fori_loop(0, tm // rows, body, 0)


def _norm_matmul_kernel(x_ref, g_ref, w_ref, o_ref, hn_ref):
    @pl.when(pl.program_id(1) == 0)
    def _():
        _norm_rows(x_ref, g_ref, hn_ref, 256)

    o_ref[...] = jnp.dot(hn_ref[...], w_ref[...], preferred_element_type=F32).astype(o_ref.dtype)


def _norm_matmul(x, g, w, *, tm, tn, out_dtype, emit_hn=False):
    m, k = x.shape
    n = w.shape[1]
    assert m % tm == 0 and n % tn == 0
    in_specs = [
        pl.BlockSpec((tm, k), lambda i, j: (i, 0)),
        pl.BlockSpec((1, k), lambda i, j: (0, 0)),
        pl.BlockSpec((k, tn), lambda i, j: (0, j)),
    ]
    o_spec = pl.BlockSpec((tm, tn), lambda i, j: (i, j))
    o_shape = jax.ShapeDtypeStruct((m, n), out_dtype)
    if emit_hn:
        out_specs = [o_spec, pl.BlockSpec((tm, k), lambda i, j: (i, 0))]
        out_shape = [o_shape, jax.ShapeDtypeStruct((m, k), BF16)]
        scratch = []
    else:
        out_specs = o_spec
        out_shape = o_shape
        scratch = [pltpu.VMEM((tm, k), BF16)]
    return pl.pallas_call(
        _norm_matmul_kernel,
        grid=(m // tm, n // tn),
        in_specs=in_specs,
        out_specs=out_specs,
        out_shape=out_shape,
        scratch_shapes=scratch,
        compiler_params=_params("parallel", "arbitrary"),
        name="norm_matmul",
    )(x, g.reshape(1, k).astype(F32), w)


def _matmul_res_kernel(a_ref, w_ref, r_ref, o_ref):
    o_ref[...] = r_ref[...] + jnp.dot(a_ref[...], w_ref[...], preferred_element_type=F32)


def _matmul_res(a, w, res, *, tm, tn):
    m, k = a.shape
    n = w.shape[1]
    assert m % tm == 0 and n % tn == 0
    return pl.pallas_call(
        _matmul_res_kernel,
        grid=(m // tm, n // tn),
        in_specs=[
            pl.BlockSpec((tm, k), lambda i, j: (i, 0)),
            pl.BlockSpec((k, tn), lambda i, j: (0, j)),
            pl.BlockSpec((tm, tn), lambda i, j: (i, j)),
        ],
        out_specs=pl.BlockSpec((tm, tn), lambda i, j: (i, j)),
        out_shape=jax.ShapeDtypeStruct((m, n), F32),
        compiler_params=_params("parallel", "parallel"),
        name="matmul_res",
    )(a, w, res)


def _tables_kernel(pos_ref, inv_ref, rc_ref, rs_ref, tc_ref, ts_ref, *, rows):
    s = pos_ref.shape[1]
    lane = lax.broadcasted_iota(jnp.int32, (1, HEAD), 1)
    half_sign = jnp.where(lane < HEAD // 2, -1.0, 1.0).astype(F32)
    pair_sign = jnp.where(lane % 2 == 0, -1.0, 1.0).astype(F32)

    def body(c, _):
        sl = pl.ds(pl.multiple_of(c * rows, rows), rows)
        pos = pos_ref[0, sl, :]
        ang = pos * inv_ref[0:1, :]
        rc_ref[0, sl, :] = jnp.cos(ang)
        rs_ref[0, sl, :] = jnp.sin(ang) * half_sign
        ang = pos * inv_ref[1:2, :]
        tc_ref[0, sl, :] = jnp.cos(ang)
        ts_ref[0, sl, :] = jnp.sin(ang) * pair_sign
        return 0

    lax.fori_loop(0, s // rows, body, 0)


def _rotation_tables(positions):
    b, s = positions.shape
    half = HEAD // 2
    inv_half = 1.0 / (ROPE_THETA ** (jnp.arange(half, dtype=F32) / half))
    inv_pair = 1.0 / (10000.0 ** jnp.linspace(0.0, 1.0, half, dtype=F32))
    inv = jnp.stack([jnp.concatenate([inv_half, inv_half]), jnp.repeat(inv_pair, 2)])
    pos = positions.astype(F32).reshape(b, s, 1)
    tab = jax.ShapeDtypeStruct((b, s, HEAD), F32)
    spec = pl.BlockSpec((1, s, HEAD), lambda i: (i, 0, 0))
    return pl.pallas_call(
        functools.partial(_tables_kernel, rows=256),
        grid=(b,),
        in_specs=[pl.BlockSpec((1, s, 1), lambda i: (i, 0, 0)),
                  pl.BlockSpec((2, HEAD), lambda i: (0, 0))],
        out_specs=[spec] * 4,
        out_shape=[tab] * 4,
        compiler_params=_params("parallel"),
        name="rotation_tables",
    )(pos, inv)


RET_HEADS_PER_STEP = 8


def _retention_kernel(logg_ref, q_ref, k_ref, v_ref, g_ref, cos_ref, sin_ref, o_ref,
                      state_ref, dmat_ref, kw_ref, qw_ref, decay_ref):
    c = RET_CHUNK
    s = q_ref.shape[1]
    hp = RET_HEADS_PER_STEP
    row = lax.broadcasted_iota(jnp.int32, (c, c), 0).astype(F32)
    col = lax.broadcasted_iota(jnp.int32, (c, c), 1).astype(F32)
    diff = row - col
    for h in range(hp):
        lg = logg_ref[pl.program_id(1) * hp + h]
        dmat_ref[h] = jnp.where(diff >= 0, jnp.exp(lg * jnp.maximum(diff, 0.0)), 0.0)
        kw_ref[h] = jnp.exp(lg * (c - 1.0 - row))
        qw_ref[h] = jnp.exp(lg * (row + 1.0))
        decay_ref[h] = jnp.exp(lg * jnp.full((c, c), float(c), F32))
    even = lax.broadcasted_iota(jnp.int32, (c, HEAD), 1) % 2 == 0
    scale = HEAD ** -0.5

    state_ref[...] = jnp.zeros_like(state_ref)

    def body(n, _):
        sl = pl.ds(pl.multiple_of(n * c, c), c)
        cos = cos_ref[0, sl, :]
        sin = sin_ref[0, sl, :]

        def rot(x):
            swapped = jnp.where(even, pltpu.roll(x, HEAD - 1, 1), pltpu.roll(x, 1, 1))
            return x * cos + swapped * sin

        heads = range(hp)
        cols = [slice(h * HEAD, (h + 1) * HEAD) for h in heads]
        q = [rot(q_ref[0, sl, cols[h]].astype(F32)) for h in heads]
        k = [rot(k_ref[0, sl, cols[h]].astype(F32)) * scale for h in heads]
        v = [v_ref[0, sl, cols[h]] for h in heads]
        state = [state_ref[h] for h in heads]
        scores = [lax.dot_general(q[h].astype(BF16), k[h].astype(BF16), NT_DIMS,
                                  preferred_element_type=F32) * dmat_ref[h] for h in heads]
        cross = [jnp.dot((q[h] * qw_ref[h]).astype(BF16), state[h].astype(BF16),
                         preferred_element_type=F32) for h in heads]
        kv = [lax.dot_general((k[h] * kw_ref[h]).astype(BF16), v[h], TN_DIMS,
                              preferred_element_type=F32) for h in heads]
        inner = [jnp.dot(scores[h].astype(BF16), v[h], preferred_element_type=F32) for h in heads]
        for h in heads:
            state_ref[h] = decay_ref[h] * state[h] + kv[h]
            out = inner[h] + cross[h]
            out = out * lax.rsqrt(jnp.mean(out * out, axis=-1, keepdims=True) + NORM_EPS)
            g = g_ref[0, sl, cols[h]].astype(F32)
            o_ref[0, sl, cols[h]] = (out * (g * jax.nn.sigmoid(g))).astype(o_ref.dtype)
        return 0

    lax.fori_loop(0, s // c, body, 0)


def _retention(proj, ret_cos, ret_sin):
    b, s, _ = proj.shape
    h = RET_HEADS
    hp = RET_HEADS_PER_STEP
    groups = h // hp
    log_g = jnp.log1p(-jnp.exp2(-5.0 - jnp.arange(h, dtype=F32)))
    heads = lambda part: pl.BlockSpec((1, s, hp * HEAD), lambda i, j: (i, 0, part * groups + j))
    tab = pl.BlockSpec((1, s, HEAD), lambda i, j: (i, 0, 0))
    per_head = pltpu.VMEM((hp, HEAD, HEAD), F32)
    return pl.pallas_call(
        _retention_kernel,
        grid=(b, groups),
        in_specs=[pl.BlockSpec(memory_space=pltpu.SMEM),
                  heads(0), heads(1), heads(2), heads(3), tab, tab],
        out_specs=pl.BlockSpec((1, s, hp * HEAD), lambda i, j: (i, 0, j)),
        out_shape=jax.ShapeDtypeStruct((b, s, h * HEAD), BF16),
        scratch_shapes=[per_head] * 5,
        compiler_params=_params("parallel", "parallel"),
        name="retention",
    )(log_g, proj, proj, proj, proj, ret_cos, ret_sin)


_DIL_WINDOWS = tuple(sorted({w // r for w, r in DIL_PATTERNS}))


def _dilated_kernel(*refs):
    qkv_refs = refs[:9]
    cos_ref, sin_ref, o_ref, qs, ks, vs, og, ol, bias_ref = refs[9:]
    s = o_ref.shape[1]
    blk = DIL_BLOCK
    scale = HEAD ** -0.5
    neg = -1e30
    qi = lax.broadcasted_iota(jnp.int32, (blk, blk), 0)
    kj = lax.broadcasted_iota(jnp.int32, (blk, blk), 1)
    cos = cos_ref[0]
    sin = sin_ref[0]

    def rot(x):
        return x * cos + pltpu.roll(x, HEAD // 2, 1) * sin

    for wi, win in enumerate(_DIL_WINDOWS):
        cur_ok = (qi - kj >= 0) & (qi - kj <= win)
        prev_ok = (qi - kj + blk) <= win
        bias_ref[2 * wi] = jnp.where(cur_ok, 0.0, neg)
        bias_ref[2 * wi + 1] = jnp.where(prev_ok, 0.0, neg)

    for g, (window, r) in enumerate(DIL_PATTERNS):
        wi = _DIL_WINDOWS.index(window // r)
        length = s // r
        nb = length // blk
        q_ref, k_ref, v_ref = qkv_refs[3 * g:3 * g + 3]
        qs[...] = rot(q_ref[0].astype(F32)) * scale
        ks[...] = rot(k_ref[0].astype(F32))
        vs[...] = v_ref[0].astype(F32)
        units = [(c, n) for c in range(r) for n in range(nb)]

        def rows(c, m):
            return pl.ds(c + m * blk * r, blk, stride=r) if r > 1 else pl.ds(m * blk, blk)

        def scores(c, n):
            q = qs[rows(c, n), :].astype(BF16)
            k1 = ks[rows(c, n), :].astype(BF16)
            s1 = lax.dot_general(q, k1, NT_DIMS, preferred_element_type=F32) + bias_ref[2 * wi]
            s0 = None
            if n > 0:
                k0 = ks[rows(c, n - 1), :].astype(BF16)
                s0 = (lax.dot_general(q, k0, NT_DIMS, preferred_element_type=F32)
                      + bias_ref[2 * wi + 1])
            return s1, s0

        def softmax(s1, s0):
            m = jnp.max(s1, axis=-1, keepdims=True)
            if s0 is not None:
                m = jnp.maximum(m, jnp.max(s0, axis=-1, keepdims=True))
            p1 = jnp.exp(s1 - m)
            l = jnp.sum(p1, axis=-1, keepdims=True)
            p0 = None
            if s0 is not None:
                p0 = jnp.exp(s0 - m)
                l = l + jnp.sum(p0, axis=-1, keepdims=True)
                p0 = p0.astype(BF16)
            return p1.astype(BF16), p0, m, l

        def finish(c, n, p1, p0, m, l):
            acc = jnp.dot(p1, vs[rows(c, n), :].astype(BF16), preferred_element_type=F32)
            if p0 is not None:
                acc = acc + jnp.dot(p0, vs[rows(c, n - 1), :].astype(BF16),
                                    preferred_element_type=F32)
            og[g, rows(c, n), :] = acc / l
            ol[g, rows(c, n), :] = jnp.broadcast_to(m + jnp.log(l), (blk, HEAD))

        sc, sm = {}, {}
        for i in range(len(units) + 2):
            if i < len(units):
                sc[i] = scores(*units[i])
            if 1 <= i <= len(units):
                sm[i - 1] = softmax(*sc.pop(i - 1))
            if i >= 2:
                finish(*units[i - 2], *sm.pop(i - 2))

    ng = len(DIL_PATTERNS)
    mx = ol[0]
    for g in range(1, ng):
        mx = jnp.maximum(mx, ol[g])
    num = jnp.zeros((s, HEAD), F32)
    den = jnp.zeros((s, HEAD), F32)
    for g in range(ng):
        e = jnp.exp(ol[g] - mx)
        num = num + e * og[g]
        den = den + e
    o_ref[0] = (num / den).astype(o_ref.dtype)


def _dilated(proj, rot_cos, rot_sin, col0):
    b, s, _ = proj.shape
    assert s % (DIL_BLOCK * max(r for _, r in DIL_PATTERNS)) == 0
    hg = DIL_HEADS_PER_GROUP
    ng = len(DIL_PATTERNS)
    width = ng * hg
    base = col0 // HEAD
    specs = []
    for g in range(ng):
        for part in range(3):
            off = base + part * width + g * hg
            specs.append(pl.BlockSpec((1, s, HEAD), lambda i, j, off=off: (i, 0, off + j)))
    tab = pl.BlockSpec((1, s, HEAD), lambda i, j: (i, 0, 0))
    return pl.pallas_call(
        _dilated_kernel,
        grid=(b, hg),
        in_specs=specs + [tab, tab],
        out_specs=pl.BlockSpec((1, s, HEAD), lambda i, j: (i, 0, j)),
        out_shape=jax.ShapeDtypeStruct((b, s, hg * HEAD), BF16),
        scratch_shapes=[pltpu.VMEM((s, HEAD), F32)] * 3
        + [pltpu.VMEM((ng, s, HEAD), F32)] * 2
        + [pltpu.VMEM((2 * len(_DIL_WINDOWS), DIL_BLOCK, DIL_BLOCK), F32)],
        compiler_params=_params("parallel", "parallel"),
        name="dilated",
    )(*([proj] * 9), rot_cos, rot_sin)


def _merge_kernel(ret_ref, dil_ref, wr_ref, wd_ref, gr_ref, gd_ref, o_ref):
    a = jnp.dot(ret_ref[...], wr_ref[...], preferred_element_type=F32)
    d = jnp.dot(dil_ref[...], wd_ref[...], preferred_element_type=F32)
    gr = jax.nn.sigmoid(gr_ref[...].astype(F32))
    gd = jax.nn.sigmoid(gd_ref[...].astype(F32))
    o_ref[...] = (gr * a + gd * d).astype(o_ref.dtype)


def _merge(ret, dil, w_br_ret, w_br_dil, proj2d, gate_col0, *, tm, tn):
    m, d_model = ret.shape[0], w_br_ret.shape[1]
    assert gate_col0 % tn == 0 and d_model % tn == 0 and m % tm == 0
    gr0 = gate_col0 // tn
    gd0 = gr0 + d_model // tn
    return pl.pallas_call(
        _merge_kernel,
        grid=(m // tm, d_model // tn),
        in_specs=[
            pl.BlockSpec((tm, ret.shape[1]), lambda i, j: (i, 0)),
            pl.BlockSpec((tm, dil.shape[1]), lambda i, j: (i, 0)),
            pl.BlockSpec((w_br_ret.shape[0], tn), lambda i, j: (0, j)),
            pl.BlockSpec((w_br_dil.shape[0], tn), lambda i, j: (0, j)),
            pl.BlockSpec((tm, tn), lambda i, j: (i, gr0 + j)),
            pl.BlockSpec((tm, tn), lambda i, j: (i, gd0 + j)),
        ],
        out_specs=pl.BlockSpec((tm, tn), lambda i, j: (i, j)),
        out_shape=jax.ShapeDtypeStruct((m, d_model), BF16),
        compiler_params=_params("parallel", "parallel"),
        name="merge",
    )(ret, dil, w_br_ret, w_br_dil, proj2d, proj2d)


def _mem_attn_kernel(q_ref, k_ref, v_ref, o_ref):
    dh = q_ref.shape[2]
    sc = lax.dot_general(q_ref[0], k_ref[0], NT_DIMS, preferred_element_type=F32) * (dh ** -0.5)
    m = jnp.max(sc, axis=-1, keepdims=True)
    p = jnp.exp(sc - m)
    l = jnp.sum(p, axis=-1, keepdims=True)
    o = jnp.dot(p.astype(BF16), v_ref[0], preferred_element_type=F32)
    o_ref[0] = (o / l).astype(o_ref.dtype)


def _mem_attention(q, kv, *, ts):
    b, s, d = q.shape
    m = kv.shape[1]
    dh = d // MEM_HEADS
    return pl.pallas_call(
        _mem_attn_kernel,
        grid=(b, s // ts, MEM_HEADS),
        in_specs=[
            pl.BlockSpec((1, ts, dh), lambda i, j, h: (i, j, h)),
            pl.BlockSpec((1, m, dh), lambda i, j, h: (i, 0, h)),
            pl.BlockSpec((1, m, dh), lambda i, j, h: (i, 0, MEM_HEADS + h)),
        ],
        out_specs=pl.BlockSpec((1, ts, dh), lambda i, j, h: (i, j, h)),
        out_shape=jax.ShapeDtypeStruct((b, s, d), BF16),
        compiler_params=_params("parallel", "parallel", "parallel"),
        name="mem_attention",
    )(q, kv, kv)


NOT_RANKED = 127.0
TOPK_COLUMNS_PER_ITER = 4


def _extract_top(ws, count, tie_break, want_rank=True):
    ws = list(ws)
    n = ws[0].shape[0]
    idx = lax.broadcasted_iota(jnp.int32, ws[0].shape, 0).astype(F32)
    wants = list(want_rank) if isinstance(want_rank, (list, tuple)) else [want_rank] * len(ws)
    ranks = [jnp.full(w.shape, NOT_RANKED, F32) if want else None for w, want in zip(ws, wants)]
    vals = [[] for _ in ws]
    for it in range(count):
        for p, w in enumerate(ws):
            m = jnp.max(w, axis=0, keepdims=True)
            if tie_break:
                first = jnp.min(jnp.where(w == m, idx, float(n)), axis=0, keepdims=True)
                hit = idx == first
            else:
                hit = w == m
            ws[p] = jnp.where(hit, -jnp.inf, w)
            if wants[p]:
                ranks[p] = jnp.where(hit, float(it), ranks[p])
            vals[p].append(m)
    return list(zip(vals, ranks, ws))


_PAIR_CANDIDATES = tuple((a, b) for a in range(PEER_TOPK) for b in range(PEER_TOPK)
                         if (a + 1) * (b + 1) <= PEER_TOPK)


def _peer_topk_kernel(q_ref, sk_ref, r2_ref, e2_ref, n1_ref, e1_ref):
    k = PEER_TOPK
    lanes = 128
    pad_rows = -len(_PAIR_CANDIDATES) % 8

    def columns(cs, tie_break):
        toks = [pl.ds(pl.multiple_of(c * lanes, lanes), lanes) for c in cs]
        sc1s = [lax.dot_general(sk_ref[0, 0], q_ref[tok, :HEAD], NT_DIMS,
                                preferred_element_type=F32) for tok in toks]
        sc2s = [lax.dot_general(sk_ref[0, 1], q_ref[tok, HEAD:], NT_DIMS,
                                preferred_element_type=F32) for tok in toks]
        stage1 = _extract_top(sc1s + sc2s, k, tie_break,
                              want_rank=[tie_break] * len(cs) + [True] * len(cs))
        first, second = stage1[:len(cs)], stage1[len(cs):]
        combos = []
        for (v1, _, _), (v2, _, _) in zip(first, second):
            rows = [v1[a] + v2[b] for a, b in _PAIR_CANDIDATES]
            rows += [jnp.full_like(rows[0], -jnp.inf)] * pad_rows
            combos.append(jnp.concatenate(rows, axis=0))
        stage2 = _extract_top(combos, k, tie_break, want_rank=False)
        off_count = None
        for j, tok in enumerate(toks):
            off = finish(tok, sc1s[j], sc2s[j], first[j], second[j], combos[j], stage2[j])
            off_count = off if off_count is None else off_count + off
        return off_count

    def finish(tok, sc1, sc2, first, second, combo, pairs):
        v1, rank1, _ = first
        v2, rank2, _ = second
        cv, _, left = pairs
        taken = jnp.where(left < combo, 1.0, 0.0)
        z = jnp.zeros_like(cv[0])
        for i in range(k):
            z = z + jnp.exp(cv[i] - cv[0])
        n1 = jnp.zeros_like(sc1)
        for a in range(k):
            rows_a = [r for r, (ca, _) in enumerate(_PAIR_CANDIDATES) if ca == a]
            cnt = taken[rows_a[0]:rows_a[0] + 1, :]
            for r in rows_a[1:]:
                cnt = cnt + taken[r:r + 1, :]
            is_a = (sc1 == v1[a]) if rank1 is None else (rank1 == float(a))
            n1 = jnp.where(is_a, cnt, n1)
        sel1 = (sc1 >= v1[k - 1]) if rank1 is None else (rank1 < NOT_RANKED)
        sel2 = rank2 < NOT_RANKED
        r2_ref[0, :, tok] = rank2.astype(BF16)
        e2_ref[0, :, tok] = jnp.where(sel2, jnp.exp(sc2 - v2[0]), 0.0).astype(BF16)
        n1_ref[0, :, tok] = n1
        e1_ref[0, :, tok] = jnp.where(sel1, jnp.exp(sc1 - v1[0]), 0.0) / z
        count = lambda flags: jnp.sum(flags, axis=0, keepdims=True)
        return (jnp.abs(count(jnp.where(sel1, 1.0, 0.0)) - k) + jnp.abs(count(jnp.where(sel2, 1.0, 0.0)) - k)
                + jnp.abs(count(taken) - k))

    def group(g, _):
        cols = [g * TOPK_COLUMNS_PER_ITER + off for off in range(TOPK_COLUMNS_PER_ITER)]
        off_count = columns(cols, tie_break=False)

        @pl.when(jnp.max(off_count) > 0.0)
        def _():
            columns(cols, tie_break=True)

        return 0

    lax.fori_loop(0, q_ref.shape[0] // (lanes * TOPK_COLUMNS_PER_ITER), group, 0)


def _peer_topk(qp, subkeys, *, tk):
    t = qp.shape[0]
    h, _, nk, dq = subkeys.shape
    assert t % tk == 0 and tk % (128 * TOPK_COLUMNS_PER_ITER) == 0
    spec = pl.BlockSpec((1, nk, tk), lambda i, j: (j, 0, i))
    narrow = jax.ShapeDtypeStruct((h, nk, t), BF16)
    wide = jax.ShapeDtypeStruct((h, nk, t), F32)
    return pl.pallas_call(
        _peer_topk_kernel,
        grid=(t // tk, h),
        in_specs=[pl.BlockSpec((tk, 2 * dq), lambda i, j: (i, j)),
                  pl.BlockSpec((1, 2, nk, dq), lambda i, j: (j, 0, 0, 0))],
        out_specs=[spec] * 4,
        out_shape=[narrow, narrow, wide, wide],
        compiler_params=_params("parallel", "parallel"),
        name="peer_topk",
    )(qp, subkeys)


BF16_SUBLANES = 16


def _row_to_bf16_tile(row, rows):
    tile = jnp.broadcast_to(row, (BF16_SUBLANES, row.shape[1])).astype(BF16)
    return jnp.concatenate([tile] * (rows // BF16_SUBLANES), axis=0)


GATE_ROUNDTRIPS = 3


def _lane_roundtrip(x):
    rows, cols = x.shape
    words = pltpu.bitcast(x, jnp.uint32)
    parts = []
    for c in range(cols // 128):
        part = words[:, c * 128:(c + 1) * 128]
        for _ in range(GATE_ROUNDTRIPS):
            part = pltpu.roll(pltpu.roll(part, 1, 1), 127, 1)
        parts.append(part)
    return pltpu.bitcast(jnp.concatenate(parts, axis=1), x.dtype)


def _peer_ffn_kernel(xn_ref, u_ref, v_ref, r2_ref, e2_ref, n1_ref, e1_ref, res_ref, gf_ref, o_ref,
                     w_ref, *, final_norm):
    j = pl.program_id(1)
    ne = u_ref.shape[0]
    nk = PEER_N_KEYS

    @pl.when(j == 0)
    def _():
        o_ref[...] = jnp.zeros_like(o_ref)

    a = lax.dot_general(u_ref[...], xn_ref[...], NT_DIMS, preferred_element_type=F32)
    for b in range(ne // nk):
        gate = None
        for h in range(PEER_HEADS):
            n1 = _row_to_bf16_tile(n1_ref[h, b:b + 1, :], nk)
            e1 = _row_to_bf16_tile(e1_ref[h, b:b + 1, :], nk)
            term = jnp.where(r2_ref[h] < n1, e2_ref[h], jnp.zeros((), BF16)) * e1
            gate = term if gate is None else gate + term
        gate = _lane_roundtrip(gate)
        ab = a[b * nk:(b + 1) * nk, :]
        act = 0.5 * ab * (1.0 + lax.erf(ab * (0.5 ** 0.5)))
        w_ref[b * nk:(b + 1) * nk, :] = act.astype(BF16) * gate
    o_ref[...] += lax.dot_general(w_ref[...], v_ref[...], TN_DIMS, preferred_element_type=F32)

    @pl.when(j == pl.num_programs(1) - 1)
    def _():
        y = res_ref[...] + o_ref[...]
        if final_norm:
            ms = jnp.mean(y * y, axis=-1, keepdims=True)
            y = (y * lax.rsqrt(ms + NORM_EPS)) * gf_ref[...]
        o_ref[...] = y


def _peer_ffn(xn, u, v, r2, e2, n1, e1, res, g_final, *, tt, ne, final_norm):
    t, d = xn.shape
    n_exp = u.shape[0]
    h, nk, _ = r2.shape
    rows = ne // nk
    assert n_exp == nk * nk and ne % nk == 0 and t % tt == 0 and n_exp % ne == 0 and rows % 8 == 0
    keys = pl.BlockSpec((h, nk, tt), lambda i, j: (0, 0, i))
    first = pl.BlockSpec((h, rows, tt), lambda i, j: (0, j, i))
    return pl.pallas_call(
        functools.partial(_peer_ffn_kernel, final_norm=final_norm),
        grid=(t // tt, n_exp // ne),
        in_specs=[
            pl.BlockSpec((tt, d), lambda i, j: (i, 0)),
            pl.BlockSpec((ne, d), lambda i, j: (j, 0)),
            pl.BlockSpec((ne, d), lambda i, j: (j, 0)),
            keys, keys, first, first,
            pl.BlockSpec((tt, d), lambda i, j: (i, 0)),
            pl.BlockSpec((1, d), lambda i, j: (0, 0)),
        ],
        out_specs=pl.BlockSpec((tt, d), lambda i, j: (i, 0)),
        out_shape=jax.ShapeDtypeStruct((t, d), F32),
        scratch_shapes=[pltpu.VMEM((ne, tt), BF16)],
        compiler_params=_params("parallel", "arbitrary"),
        name="peer_ffn",
    )(xn, u, v, r2, e2, n1, e1, res, g_final.reshape(1, d).astype(F32))


def kernel(x, mem, positions, g_mix, w_in, w_br_ret, w_br_dil, w_out, g_cross, g_mem, w_q_mem,
           w_kv_mem, w_o_mem, g_ffn, w_peer_q, peer_subkeys, peer_u, peer_v, g_final):
    b, s, d = x.shape
    t = b * s
    depth = w_in.shape[0]
    ret_w = RET_HEADS * HEAD
    dil_col0 = 4 * ret_w
    dil_w = len(DIL_PATTERNS) * DIL_HEADS_PER_GROUP * HEAD
    gate_col0 = dil_col0 + 3 * dil_w
    mem2d = mem.reshape(-1, d)
    tl = _tiles(t, s, mem2d.shape[0])

    rot_cos, rot_sin, ret_cos, ret_sin = _rotation_tables(positions)
    x2d = x.reshape(t, d)
    for l in range(depth):
        bf = lambda w: w[l].astype(BF16)
        proj = _norm_matmul(x2d, g_mix[l], bf(w_in), tm=tl.rows, tn=tl.proj_cols, out_dtype=BF16)
        proj3 = proj.reshape(b, s, -1)
        ret = _retention(proj3, ret_cos, ret_sin).reshape(t, ret_w)
        dil = _dilated(proj3, rot_cos, rot_sin, dil_col0).reshape(t, -1)
        merged = _merge(ret, dil, bf(w_br_ret), bf(w_br_dil), proj, gate_col0, tm=tl.merge_rows,
                        tn=tl.merge_cols)
        x2d = _matmul_res(merged, bf(w_out), x2d, tm=tl.rows, tn=tl.cols)
        kv = _norm_matmul(mem2d, g_mem[l], bf(w_kv_mem), tm=tl.mem_rows, tn=tl.norm_cols,
                          out_dtype=BF16)
        qm = _norm_matmul(x2d, g_cross[l], bf(w_q_mem), tm=tl.rows, tn=tl.norm_cols, out_dtype=BF16)
        att = _mem_attention(qm.reshape(b, s, d), kv.reshape(b, -1, 2 * d), ts=tl.attn_rows)
        x2d = _matmul_res(att.reshape(t, d), bf(w_o_mem), x2d, tm=tl.rows, tn=tl.cols)
        qp, xn = _norm_matmul(x2d, g_ffn[l], bf(w_peer_q), tm=tl.rows, tn=tl.norm_cols,
                              out_dtype=BF16, emit_hn=True)
        r2, e2, n1, e1 = _peer_topk(qp, bf(peer_subkeys), tk=tl.topk_tokens)
        x2d = _peer_ffn(xn, bf(peer_u), bf(peer_v), r2, e2, n1, e1, x2d, g_final,
                        tt=tl.ffn_tokens, ne=tl.ffn_experts, final_norm=(l == depth - 1))
    return x2d.reshape(b, s, d)
```

```python
import functools
from typing import NamedTuple

import jax
import jax.numpy as jnp
from jax import lax
from jax.experimental import pallas as pl
from jax.experimental.pallas import tpu as pltpu

F32 = jnp.float32
BF16 = jnp.bfloat16

NORM_EPS = 1e-6
ROPE_THETA = 10000.0
HEAD = 128
RET_HEADS = 8
RET_CHUNK = 128
DIL_PATTERNS = ((128, 1), (512, 4), (2048, 16))
DIL_HEADS_PER_GROUP = 4
DIL_BLOCK = 128
MEM_HEADS = 4
PEER_HEADS = 8
PEER_N_KEYS = 128
PEER_TOPK = 16

V7X_VMEM_BYTES = 64 * 1024 * 1024
COMPILER_RESERVE_BYTES = 8 * 1024 * 1024
VMEM_LIMIT = V7X_VMEM_BYTES - COMPILER_RESERVE_BYTES

NT_DIMS = (((1,), (1,)), ((), ()))
TN_DIMS = (((0,), (0,)), ((), ()))


class Tiles(NamedTuple):
    rows: int
    mem_rows: int
    proj_cols: int
    norm_cols: int
    cols: int
    merge_rows: int
    merge_cols: int
    attn_rows: int
    topk_tokens: int
    ffn_tokens: int
    ffn_experts: int


def _tiles(t, s, mem_rows):
    return Tiles(rows=min(1024, t), mem_rows=min(1024, mem_rows), proj_cols=2560, norm_cols=2048,
                 cols=1024, merge_rows=min(2048, t), merge_cols=512, attn_rows=min(2048, s),
                 topk_tokens=min(2048, t), ffn_tokens=512, ffn_experts=1024)


def _params(*sem):
    return pltpu.CompilerParams(dimension_semantics=sem, vmem_limit_bytes=VMEM_LIMIT)


def _norm_rows(x_ref, g_ref, hn_ref, rows):
    tm = x_ref.shape[0]

    def body(c, _):
        sl = pl.ds(pl.multiple_of(c * rows, rows), rows)
        x = x_ref[sl, :].astype(F32)
        ms = jnp.mean(x * x, axis=-1, keepdims=True)
        hn_ref[sl, :] = ((x * lax.rsqrt(ms + NORM_EPS)) * g_ref[...]).astype(BF16)
        return 0

    lax.fori_loop(0, tm // rows, body, 0)


def _norm_matmul_kernel(x_ref, g_ref, w_ref, o_ref, hn_ref):
    @pl.when(pl.program_id(1) == 0)
    def _():
        _norm_rows(x_ref, g_ref, hn_ref, 256)

    o_ref[...] = jnp.dot(hn_ref[...], w_ref[...], preferred_element_type=F32).astype(o_ref.dtype)


def _norm_matmul(x, g, w, *, tm, tn, out_dtype, emit_hn=False):
    m, k = x.shape
    n = w.shape[1]
    assert m % tm == 0 and n % tn == 0
    in_specs = [
        pl.BlockSpec((tm, k), lambda i, j: (i, 0)),
        pl.BlockSpec((1, k), lambda i, j: (0, 0)),
        pl.BlockSpec((k, tn), lambda i, j: (0, j)),
    ]
    o_spec = pl.BlockSpec((tm, tn), lambda i, j: (i, j))
    o_shape = jax.ShapeDtypeStruct((m, n), out_dtype)
    if emit_hn:
        out_specs = [o_spec, pl.BlockSpec((tm, k), lambda i, j: (i, 0))]
        out_shape = [o_shape, jax.ShapeDtypeStruct((m, k), BF16)]
        scratch = []
    else:
        out_specs = o_spec
        out_shape = o_shape
        scratch = [pltpu.VMEM((tm, k), BF16)]
    return pl.pallas_call(
        _norm_matmul_kernel,
        grid=(m // tm, n // tn),
        in_specs=in_specs,
        out_specs=out_specs,
        out_shape=out_shape,
        scratch_shapes=scratch,
        compiler_params=_params("parallel", "arbitrary"),
        name="norm_matmul",
    )(x, g.reshape(1, k).astype(F32), w)


def _matmul_res_kernel(a_ref, w_ref, r_ref, o_ref):
    o_ref[...] = r_ref[...] + jnp.dot(a_ref[...], w_ref[...], preferred_element_type=F32)


def _matmul_res(a, w, res, *, tm, tn):
    m, k = a.shape
    n = w.shape[1]
    assert m % tm == 0 and n % tn == 0
    return pl.pallas_call(
        _matmul_res_kernel,
        grid=(m // tm, n // tn),
        in_specs=[
            pl.BlockSpec((tm, k), lambda i, j: (i, 0)),
            pl.BlockSpec((k, tn), lambda i, j: (0, j)),
            pl.BlockSpec((tm, tn), lambda i, j: (i, j)),
        ],
        out_specs=pl.BlockSpec((tm, tn), lambda i, j: (i, j)),
        out_shape=jax.ShapeDtypeStruct((m, n), F32),
        compiler_params=_params("parallel", "parallel"),
        name="matmul_res",
    )(a, w, res)


def _tables_kernel(pos_ref, inv_ref, rc_ref, rs_ref, tc_ref, ts_ref, *, rows):
    s = pos_ref.shape[1]
    lane = lax.broadcasted_iota(jnp.int32, (1, HEAD), 1)
    half_sign = jnp.where(lane < HEAD // 2, -1.0, 1.0).astype(F32)
    pair_sign = jnp.where(lane % 2 == 0, -1.0, 1.0).astype(F32)

    first_half = lane < HEAD // 2
    even = lane % 2 == 0
    half = rows // 2

    def body(c, _):
        sl_a = pl.ds(pl.multiple_of(c * rows, half), half)
        sl_b = pl.ds(pl.multiple_of(c * rows + half, half), half)
        pos_a = pos_ref[0, sl_a, :]
        pos_b = pos_ref[0, sl_b, :]
        ang = jnp.where(first_half, pos_a, pos_b) * inv_ref[0:1, :]
        for ref, val in ((rc_ref, jnp.cos(ang)), (rs_ref, jnp.sin(ang))):
            swapped = pltpu.roll(val, HEAD // 2, 1)
            sign = half_sign if ref is rs_ref else 1.0
            ref[0, sl_a, :] = jnp.where(first_half, val, swapped) * sign
            ref[0, sl_b, :] = jnp.where(first_half, swapped, val) * sign
        ang = jnp.where(even, pos_a, pos_b) * inv_ref[1:2, :]
        for ref, val in ((tc_ref, jnp.cos(ang)), (ts_ref, jnp.sin(ang))):
            sign = pair_sign if ref is ts_ref else 1.0
            ref[0, sl_a, :] = jnp.where(even, val, pltpu.roll(val, 1, 1)) * sign
            ref[0, sl_b, :] = jnp.where(even, pltpu.roll(val, HEAD - 1, 1), val) * sign
        return 0

    lax.fori_loop(0, s // rows, body, 0)


def _rotation_tables(positions):
    b, s = positions.shape
    half = HEAD // 2
    inv_half = 1.0 / (ROPE_THETA ** (jnp.arange(half, dtype=F32) / half))
    inv_pair = 1.0 / (10000.0 ** jnp.linspace(0.0, 1.0, half, dtype=F32))
    inv = jnp.stack([jnp.concatenate([inv_half, inv_half]), jnp.repeat(inv_pair, 2)])
    pos = positions.astype(F32).reshape(b, s, 1)
    tab = jax.ShapeDtypeStruct((b, s, HEAD), F32)
    spec = pl.BlockSpec((1, s, HEAD), lambda i: (i, 0, 0))
    return pl.pallas_call(
        functools.partial(_tables_kernel, rows=256),
        grid=(b,),
        in_specs=[pl.BlockSpec((1, s, 1), lambda i: (i, 0, 0)),
                  pl.BlockSpec((2, HEAD), lambda i: (0, 0))],
        out_specs=[spec] * 4,
        out_shape=[tab] * 4,
        compiler_params=_params("parallel"),
        name="rotation_tables",
    )(pos, inv)


RET_HEADS_PER_STEP = 8


def _retention_kernel(logg_ref, q_ref, k_ref, v_ref, g_ref, cos_ref, sin_ref, o_ref,
                      state_ref, dmat_ref, kw_ref, qw_ref, decay_ref):
    c = RET_CHUNK
    s = q_ref.shape[1]
    hp = RET_HEADS_PER_STEP
    row = lax.broadcasted_iota(jnp.int32, (c, c), 0).astype(F32)
    col = lax.broadcasted_iota(jnp.int32, (c, c), 1).astype(F32)
    diff = row - col
    for h in range(hp):
        lg = logg_ref[pl.program_id(1) * hp + h]
        dmat_ref[h] = jnp.where(diff >= 0, jnp.exp(lg * jnp.maximum(diff, 0.0)), 0.0)
        kw_ref[h] = jnp.exp(lg * (c - 1.0 - row))
        qw_ref[h] = jnp.exp(lg * (row + 1.0))
        decay_ref[h] = jnp.exp(lg * jnp.full((c, c), float(c), F32))
    even = lax.broadcasted_iota(jnp.int32, (c, HEAD), 1) % 2 == 0
    scale = HEAD ** -0.5

    state_ref[...] = jnp.zeros_like(state_ref)

    def body(n, _):
        sl = pl.ds(pl.multiple_of(n * c, c), c)
        cos = cos_ref[0, sl, :]
        sin = sin_ref[0, sl, :]

        def rot(x):
            swapped = jnp.where(even, pltpu.roll(x, HEAD - 1, 1), pltpu.roll(x, 1, 1))
            return x * cos + swapped * sin

        heads = range(hp)
        cols = [slice(h * HEAD, (h + 1) * HEAD) for h in heads]
        q = [rot(q_ref[0, sl, cols[h]].astype(F32)) for h in heads]
        k = [rot(k_ref[0, sl, cols[h]].astype(F32)) * scale for h in heads]
        v = [v_ref[0, sl, cols[h]] for h in heads]
        state = [state_ref[h] for h in heads]
        scores = [lax.dot_general(q[h].astype(BF16), k[h].astype(BF16), NT_DIMS,
                                  preferred_element_type=F32) * dmat_ref[h] for h in heads]
        cross = [jnp.dot((q[h] * qw_ref[h]).astype(BF16), state[h].astype(BF16),
                         preferred_element_type=F32) for h in heads]
        kv = [lax.dot_general((k[h] * kw_ref[h]).astype(BF16), v[h], TN_DIMS,
                              preferred_element_type=F32) for h in heads]
        inner = [jnp.dot(scores[h].astype(BF16), v[h], preferred_element_type=F32) for h in heads]
        for h in heads:
            state_ref[h] = decay_ref[h] * state[h] + kv[h]
            out = inner[h] + cross[h]
            out = out * lax.rsqrt(jnp.mean(out * out, axis=-1, keepdims=True) + NORM_EPS)
            g = g_ref[0, sl, cols[h]].astype(F32)
            o_ref[0, sl, cols[h]] = (out * (g * jax.nn.sigmoid(g))).astype(o_ref.dtype)
        return 0

    lax.fori_loop(0, s // c, body, 0)


def _retention(proj, ret_cos, ret_sin):
    b, s, _ = proj.shape
    h = RET_HEADS
    hp = RET_HEADS_PER_STEP
    groups = h // hp
    log_g = jnp.log1p(-jnp.exp2(-5.0 - jnp.arange(h, dtype=F32)))
    heads = lambda part: pl.BlockSpec((1, s, hp * HEAD), lambda i, j: (i, 0, part * groups + j))
    tab = pl.BlockSpec((1, s, HEAD), lambda i, j: (i, 0, 0))
    per_head = pltpu.VMEM((hp, HEAD, HEAD), F32)
    return pl.pallas_call(
        _retention_kernel,
        grid=(b, groups),
        in_specs=[pl.BlockSpec(memory_space=pltpu.SMEM),
                  heads(0), heads(1), heads(2), heads(3), tab, tab],
        out_specs=pl.BlockSpec((1, s, hp * HEAD), lambda i, j: (i, 0, j)),
        out_shape=jax.ShapeDtypeStruct((b, s, h * HEAD), BF16),
        scratch_shapes=[per_head] * 5,
        compiler_params=_params("parallel", "parallel"),
        name="retention",
    )(log_g, proj, proj, proj, proj, ret_cos, ret_sin)


_DIL_WINDOWS = tuple(sorted({w // r for w, r in DIL_PATTERNS}))


def _dilated_kernel(*refs):
    qkv_refs = refs[:9]
    cos_ref, sin_ref, o_ref, qs, ks, vs, og, ol, bias_ref = refs[9:]
    s = o_ref.shape[1]
    blk = DIL_BLOCK
    scale = HEAD ** -0.5
    neg = -1e30
    qi = lax.broadcasted_iota(jnp.int32, (blk, blk), 0)
    kj = lax.broadcasted_iota(jnp.int32, (blk, blk), 1)
    cos = cos_ref[0]
    sin = sin_ref[0]

    def rot(x):
        return x * cos + pltpu.roll(x, HEAD // 2, 1) * sin

    for wi, win in enumerate(_DIL_WINDOWS):
        cur_ok = (qi - kj >= 0) & (qi - kj <= win)
        prev_ok = (qi - kj + blk) <= win
        bias_ref[2 * wi] = jnp.where(cur_ok, 0.0, neg)
        bias_ref[2 * wi + 1] = jnp.where(prev_ok, 0.0, neg)

    for g, (window, r) in enumerate(DIL_PATTERNS):
        wi = _DIL_WINDOWS.index(window // r)
        length = s // r
        nb = length // blk
        q_ref, k_ref, v_ref = qkv_refs[3 * g:3 * g + 3]
        qs[...] = rot(q_ref[0].astype(F32)) * scale
        ks[...] = rot(k_ref[0].astype(F32))
        vs[...] = v_ref[0].astype(F32)
        units = [(c, n) for c in range(r) for n in range(nb)]

        def rows(c, m):
            return pl.ds(c + m * blk * r, blk, stride=r) if r > 1 else pl.ds(m * blk, blk)

        def scores(c, n):
            q = qs[rows(c, n), :].astype(BF16)
            k1 = ks[rows(c, n), :].astype(BF16)
            s1 = lax.dot_general(q, k1, NT_DIMS, preferred_element_type=F32) + bias_ref[2 * wi]
            s0 = None
            if n > 0:
                k0 = ks[rows(c, n - 1), :].astype(BF16)
                s0 = (lax.dot_general(q, k0, NT_DIMS, preferred_element_type=F32)
                      + bias_ref[2 * wi + 1])
            return s1, s0

        def softmax(s1, s0):
            m = jnp.max(s1, axis=-1, keepdims=True)
            if s0 is not None:
                m = jnp.maximum(m, jnp.max(s0, axis=-1, keepdims=True))
            p1 = jnp.exp(s1 - m)
            l = jnp.sum(p1, axis=-1, keepdims=True)
            p0 = None
            if s0 is not None:
                p0 = jnp.exp(s0 - m)
                l = l + jnp.sum(p0, axis=-1, keepdims=True)
                p0 = p0.astype(BF16)
            return p1.astype(BF16), p0, m, l

        def finish(c, n, p1, p0, m, l):
            acc = jnp.dot(p1, vs[rows(c, n), :].astype(BF16), preferred_element_type=F32)
            if p0 is not None:
                acc = acc + jnp.dot(p0, vs[rows(c, n - 1), :].astype(BF16),
                                    preferred_element_type=F32)
            og[g, rows(c, n), :] = acc / l
            ol[g, rows(c, n), :] = jnp.broadcast_to(m + jnp.log(l), (blk, HEAD))

        sc, sm = {}, {}
        for i in range(len(units) + 2):
            if i < len(units):
                sc[i] = scores(*units[i])
            if 1 <= i <= len(units):
                sm[i - 1] = softmax(*sc.pop(i - 1))
            if i >= 2:
                finish(*units[i - 2], *sm.pop(i - 2))

    ng = len(DIL_PATTERNS)
    mx = ol[0]
    for g in range(1, ng):
        mx = jnp.maximum(mx, ol[g])
    num = jnp.zeros((s, HEAD), F32)
    den = jnp.zeros((s, HEAD), F32)
    for g in range(ng):
        e = jnp.exp(ol[g] - mx)
        num = num + e * og[g]
        den = den + e
    o_ref[0] = (num / den).astype(o_ref.dtype)


def _dilated(proj, rot_cos, rot_sin, col0):
    b, s, _ = proj.shape
    assert s % (DIL_BLOCK * max(r for _, r in DIL_PATTERNS)) == 0
    hg = DIL_HEADS_PER_GROUP
    ng = len(DIL_PATTERNS)
    width = ng * hg
    base = col0 // HEAD
    specs = []
    for g in range(ng):
        for part in range(3):
            off = base + part * width + g * hg
            specs.append(pl.BlockSpec((1, s, HEAD), lambda i, j, off=off: (i, 0, off + j)))
    tab = pl.BlockSpec((1, s, HEAD), lambda i, j: (i, 0, 0))
    return pl.pallas_call(
        _dilated_kernel,
        grid=(b, hg),
        in_specs=specs + [tab, tab],
        out_specs=pl.BlockSpec((1, s, HEAD), lambda i, j: (i, 0, j)),
        out_shape=jax.ShapeDtypeStruct((b, s, hg * HEAD), BF16),
        scratch_shapes=[pltpu.VMEM((s, HEAD), F32)] * 3
        + [pltpu.VMEM((ng, s, HEAD), F32)] * 2
        + [pltpu.VMEM((2 * len(_DIL_WINDOWS), DIL_BLOCK, DIL_BLOCK), F32)],
        compiler_params=_params("parallel", "parallel"),
        name="dilated",
    )(*([proj] * 9), rot_cos, rot_sin)


def _merge_kernel(ret_ref, dil_ref, wr_ref, wd_ref, gr_ref, gd_ref, o_ref):
    a = jnp.dot(ret_ref[...], wr_ref[...], preferred_element_type=F32)
    d = jnp.dot(dil_ref[...], wd_ref[...], preferred_element_type=F32)
    gr = jax.nn.sigmoid(gr_ref[...].astype(F32))
    gd = jax.nn.sigmoid(gd_ref[...].astype(F32))
    o_ref[...] = (gr * a + gd * d).astype(o_ref.dtype)


def _merge(ret, dil, w_br_ret, w_br_dil, proj2d, gate_col0, *, tm, tn):
    m, d_model = ret.shape[0], w_br_ret.shape[1]
    assert gate_col0 % tn == 0 and d_model % tn == 0 and m % tm == 0
    gr0 = gate_col0 // tn
    gd0 = gr0 + d_model // tn
    return pl.pallas_call(
        _merge_kernel,
        grid=(m // tm, d_model // tn),
        in_specs=[
            pl.BlockSpec((tm, ret.shape[1]), lambda i, j: (i, 0)),
            pl.BlockSpec((tm, dil.shape[1]), lambda i, j: (i, 0)),
            pl.BlockSpec((w_br_ret.shape[0], tn), lambda i, j: (0, j)),
            pl.BlockSpec((w_br_dil.shape[0], tn), lambda i, j: (0, j)),
            pl.BlockSpec((tm, tn), lambda i, j: (i, gr0 + j)),
            pl.BlockSpec((tm, tn), lambda i, j: (i, gd0 + j)),
        ],
        out_specs=pl.BlockSpec((tm, tn), lambda i, j: (i, j)),
        out_shape=jax.ShapeDtypeStruct((m, d_model), BF16),
        compiler_params=_params("parallel", "parallel"),
        name="merge",
    )(ret, dil, w_br_ret, w_br_dil, proj2d, proj2d)


def _mem_attn_kernel(q_ref, k_ref, v_ref, o_ref):
    dh = q_ref.shape[2]
    sc = lax.dot_general(q_ref[0], k_ref[0], NT_DIMS, preferred_element_type=F32) * (dh ** -0.5)
    m = jnp.max(sc, axis=-1, keepdims=True)
    p = jnp.exp(sc - m)
    l = jnp.sum(p, axis=-1, keepdims=True)
    o = jnp.dot(p.astype(BF16), v_ref[0], preferred_element_type=F32)
    o_ref[0] = (o / l).astype(o_ref.dtype)


def _mem_attention(q, kv, *, ts):
    b, s, d = q.shape
    m = kv.shape[1]
    dh = d // MEM_HEADS
    return pl.pallas_call(
        _mem_attn_kernel,
        grid=(b, s // ts, MEM_HEADS),
        in_specs=[
            pl.BlockSpec((1, ts, dh), lambda i, j, h: (i, j, h)),
            pl.BlockSpec((1, m, dh), lambda i, j, h: (i, 0, h)),
            pl.BlockSpec((1, m, dh), lambda i, j, h: (i, 0, MEM_HEADS + h)),
        ],
        out_specs=pl.BlockSpec((1, ts, dh), lambda i, j, h: (i, j, h)),
        out_shape=jax.ShapeDtypeStruct((b, s, d), BF16),
        compiler_params=_params("parallel", "parallel", "parallel"),
        name="mem_attention",
    )(q, kv, kv)


NOT_RANKED = 127.0
TOPK_COLUMNS_PER_ITER = 4


def _extract_top(ws, count, tie_break, want_rank=True):
    ws = list(ws)
    n = ws[0].shape[0]
    idx = lax.broadcasted_iota(jnp.int32, ws[0].shape, 0).astype(F32)
    wants = list(want_rank) if isinstance(want_rank, (list, tuple)) else [want_rank] * len(ws)
    ranks = [jnp.full(w.shape, NOT_RANKED, F32) if want else None for w, want in zip(ws, wants)]
    vals = [[] for _ in ws]
    for it in range(count):
        for p, w in enumerate(ws):
            m = jnp.max(w, axis=0, keepdims=True)
            if tie_break:
                first = jnp.min(jnp.where(w == m, idx, float(n)), axis=0, keepdims=True)
                hit = idx == first
            else:
                hit = w == m
            ws[p] = jnp.where(hit, -jnp.inf, w)
            if wants[p]:
                ranks[p] = jnp.where(hit, float(it), ranks[p])
            vals[p].append(m)
    return list(zip(vals, ranks, ws))


_PAIR_CANDIDATES = tuple((a, b) for a in range(PEER_TOPK) for b in range(PEER_TOPK)
                         if (a + 1) * (b + 1) <= PEER_TOPK)


def _peer_topk_kernel(q_ref, sk_ref, r2_ref, e2_ref, n1_ref, e1_ref):
    k = PEER_TOPK
    lanes = 128
    pad_rows = -len(_PAIR_CANDIDATES) % 8

    def columns(cs, tie_break):
        toks = [pl.ds(pl.multiple_of(c * lanes, lanes), lanes) for c in cs]
        sc1s = [lax.dot_general(sk_ref[0, 0], q_ref[tok, :HEAD], NT_DIMS,
                                preferred_element_type=F32) for tok in toks]
        sc2s = [lax.dot_general(sk_ref[0, 1], q_ref[tok, HEAD:], NT_DIMS,
                                preferred_element_type=F32) for tok in toks]
        stage1 = _extract_top(sc1s + sc2s, k, tie_break,
                              want_rank=[tie_break] * len(cs) + [True] * len(cs))
        first, second = stage1[:len(cs)], stage1[len(cs):]
        combos = []
        for (v1, _, _), (v2, _, _) in zip(first, second):
            rows = [v1[a] + v2[b] for a, b in _PAIR_CANDIDATES]
            rows += [jnp.full_like(rows[0], -jnp.inf)] * pad_rows
            combos.append(jnp.concatenate(rows, axis=0))
        stage2 = _extract_top(combos, k, tie_break, want_rank=False)
        off_count = None
        for j, tok in enumerate(toks):
            off = finish(tok, sc1s[j], sc2s[j], first[j], second[j], combos[j], stage2[j])
            off_count = off if off_count is None else off_count + off
        return off_count

    def finish(tok, sc1, sc2, first, second, combo, pairs):
        v1, rank1, _ = first
        v2, rank2, _ = second
        cv, _, left = pairs
        taken = jnp.where(left < combo, 1.0, 0.0)
        z = jnp.zeros_like(cv[0])
        for i in range(k):
            z = z + jnp.exp(cv[i] - cv[0])
        n1 = jnp.zeros_like(sc1)
        for a in range(k):
            rows_a = [r for r, (ca, _) in enumerate(_PAIR_CANDIDATES) if ca == a]
            cnt = taken[rows_a[0]:rows_a[0] + 1, :]
            for r in rows_a[1:]:
                cnt = cnt + taken[r:r + 1, :]
            is_a = (sc1 == v1[a]) if rank1 is None else (rank1 == float(a))
            n1 = jnp.where(is_a, cnt, n1)
        sel1 = (sc1 >= v1[k - 1]) if rank1 is None else (rank1 < NOT_RANKED)
        sel2 = rank2 < NOT_RANKED
        r2_ref[0, :, tok] = rank2.astype(BF16)
        e2_ref[0, :, tok] = jnp.where(sel2, jnp.exp(sc2 - v2[0]), 0.0).astype(BF16)
        n1_ref[0, :, tok] = n1
        e1_ref[0, :, tok] = jnp.where(sel1, jnp.exp(sc1 - v1[0]), 0.0) / z
        count = lambda flags: jnp.sum(flags, axis=0, keepdims=True)
        return (jnp.abs(count(jnp.where(sel1, 1.0, 0.0)) - k) + jnp.abs(count(jnp.where(sel2, 1.0, 0.0)) - k)
                + jnp.abs(count(taken) - k))

    def group(g, _):
        cols = [g * TOPK_COLUMNS_PER_ITER + off for off in range(TOPK_COLUMNS_PER_ITER)]
        off_count = columns(cols, tie_break=False)

        @pl.when(jnp.max(off_count) > 0.0)
        def _():
            columns(cols, tie_break=True)

        return 0

    lax.fori_loop(0, q_ref.shape[0] // (lanes * TOPK_COLUMNS_PER_ITER), group, 0)


def _peer_topk(qp, subkeys, *, tk):
    t = qp.shape[0]
    h, _, nk, dq = subkeys.shape
    assert t % tk == 0 and tk % (128 * TOPK_COLUMNS_PER_ITER) == 0
    spec = pl.BlockSpec((1, nk, tk), lambda i, j: (j, 0, i))
    narrow = jax.ShapeDtypeStruct((h, nk, t), BF16)
    wide = jax.ShapeDtypeStruct((h, nk, t), F32)
    return pl.pallas_call(
        _peer_topk_kernel,
        grid=(t // tk, h),
        in_specs=[pl.BlockSpec((tk, 2 * dq), lambda i, j: (i, j)),
                  pl.BlockSpec((1, 2, nk, dq), lambda i, j: (j, 0, 0, 0))],
        out_specs=[spec] * 4,
        out_shape=[narrow, narrow, wide, wide],
        compiler_params=_params("parallel", "parallel"),
        name="peer_topk",
    )(qp, subkeys)


BF16_SUBLANES = 16


def _row_to_bf16_tile(row, rows):
    tile = jnp.broadcast_to(row, (BF16_SUBLANES, row.shape[1])).astype(BF16)
    return jnp.concatenate([tile] * (rows // BF16_SUBLANES), axis=0)


GATE_ROUNDTRIPS = 3


def _lane_roundtrip(x):
    rows, cols = x.shape
    words = pltpu.bitcast(x, jnp.uint32)
    parts = []
    for c in range(cols // 128):
        part = words[:, c * 128:(c + 1) * 128]
        for _ in range(GATE_ROUNDTRIPS):
            part = pltpu.roll(pltpu.roll(part, 1, 1), 127, 1)
        parts.append(part)
    return pltpu.bitcast(jnp.concatenate(parts, axis=1), x.dtype)


def _peer_ffn_kernel(xn_ref, u_ref, v_ref, r2_ref, e2_ref, n1_ref, e1_ref, res_ref, gf_ref, o_ref,
                     w_ref, *, final_norm):
    j = pl.program_id(1)
    ne = u_ref.shape[0]
    nk = PEER_N_KEYS

    @pl.when(j == 0)
    def _():
        o_ref[...] = jnp.zeros_like(o_ref)

    a = lax.dot_general(u_ref[...], xn_ref[...], NT_DIMS, preferred_element_type=F32)
    for b in range(ne // nk):
        gate = None
        for h in range(PEER_HEADS):
            n1 = _row_to_bf16_tile(n1_ref[h, b:b + 1, :], nk)
            e1 = _row_to_bf16_tile(e1_ref[h, b:b + 1, :], nk)
            term = jnp.where(r2_ref[h] < n1, e2_ref[h], jnp.zeros((), BF16)) * e1
            gate = term if gate is None else gate + term
        gate = _lane_roundtrip(gate)
        ab = a[b * nk:(b + 1) * nk, :]
        act = 0.5 * ab * (1.0 + lax.erf(ab * (0.5 ** 0.5)))
        w_ref[b * nk:(b + 1) * nk, :] = act.astype(BF16) * gate
    o_ref[...] += lax.dot_general(w_ref[...], v_ref[...], TN_DIMS, preferred_element_type=F32)

    @pl.when(j == pl.num_programs(1) - 1)
    def _():
        y = res_ref[...] + o_ref[...]
        if final_norm:
            ms = jnp.mean(y * y, axis=-1, keepdims=True)
            y = (y * lax.rsqrt(ms + NORM_EPS)) * gf_ref[...]
        o_ref[...] = y


def _peer_ffn(xn, u, v, r2, e2, n1, e1, res, g_final, *, tt, ne, final_norm):
    t, d = xn.shape
    n_exp = u.shape[0]
    h, nk, _ = r2.shape
    rows = ne // nk
    assert n_exp == nk * nk and ne % nk == 0 and t % tt == 0 and n_exp % ne == 0 and rows % 8 == 0
    keys = pl.BlockSpec((h, nk, tt), lambda i, j: (0, 0, i))
    first = pl.BlockSpec((h, rows, tt), lambda i, j: (0, j, i))
    return pl.pallas_call(
        functools.partial(_peer_ffn_kernel, final_norm=final_norm),
        grid=(t // tt, n_exp // ne),
        in_specs=[
            pl.BlockSpec((tt, d), lambda i, j: (i, 0)),
            pl.BlockSpec((ne, d), lambda i, j: (j, 0)),
            pl.BlockSpec((ne, d), lambda i, j: (j, 0)),
            keys, keys, first, first,
            pl.BlockSpec((tt, d), lambda i, j: (i, 0)),
            pl.BlockSpec((1, d), lambda i, j: (0, 0)),
        ],
        out_specs=pl.BlockSpec((tt, d), lambda i, j: (i, 0)),
        out_shape=jax.ShapeDtypeStruct((t, d), F32),
        scratch_shapes=[pltpu.VMEM((ne, tt), BF16)],
        compiler_params=_params("parallel", "arbitrary"),
        name="peer_ffn",
    )(xn, u, v, r2, e2, n1, e1, res, g_final.reshape(1, d).astype(F32))


def kernel(x, mem, positions, g_mix, w_in, w_br_ret, w_br_dil, w_out, g_cross, g_mem, w_q_mem,
           w_kv_mem, w_o_mem, g_ffn, w_peer_q, peer_subkeys, peer_u, peer_v, g_final):
    b, s, d = x.shape
    t = b * s
    depth = w_in.shape[0]
    ret_w = RET_HEADS * HEAD
    dil_col0 = 4 * ret_w
    dil_w = len(DIL_PATTERNS) * DIL_HEADS_PER_GROUP * HEAD
    gate_col0 = dil_col0 + 3 * dil_w
    mem2d = mem.reshape(-1, d)
    tl = _tiles(t, s, mem2d.shape[0])

    rot_cos, rot_sin, ret_cos, ret_sin = _rotation_tables(positions)
    x2d = x.reshape(t, d)
    for l in range(depth):
        bf = lambda w: w[l].astype(BF16)
        proj = _norm_matmul(x2d, g_mix[l], bf(w_in), tm=tl.rows, tn=tl.proj_cols, out_dtype=BF16)
        proj3 = proj.reshape(b, s, -1)
        ret = _retention(proj3, ret_cos, ret_sin).reshape(t, ret_w)
        dil = _dilated(proj3, rot_cos, rot_sin, dil_col0).reshape(t, -1)
        merged = _merge(ret, dil, bf(w_br_ret), bf(w_br_dil), proj, gate_col0, tm=tl.merge_rows,
                        tn=tl.merge_cols)
        x2d = _matmul_res(merged, bf(w_out), x2d, tm=tl.rows, tn=tl.cols)
        kv = _norm_matmul(mem2d, g_mem[l], bf(w_kv_mem), tm=tl.mem_rows, tn=tl.norm_cols,
                          out_dtype=BF16)
        qm = _norm_matmul(x2d, g_cross[l], bf(w_q_mem), tm=tl.rows, tn=tl.norm_cols, out_dtype=BF16)
        att = _mem_attention(qm.reshape(b, s, d), kv.reshape(b, -1, 2 * d), ts=tl.attn_rows)
        x2d = _matmul_res(att.reshape(t, d), bf(w_o_mem), x2d, tm=tl.rows, tn=tl.cols)
        qp, xn = _norm_matmul(x2d, g_ffn[l], bf(w_peer_q), tm=tl.rows, tn=tl.norm_cols,
                              out_dtype=BF16, emit_hn=True)
        r2, e2, n1, e1 = _peer_topk(qp, bf(peer_subkeys), tk=tl.topk_tokens)
        x2d = _peer_ffn(xn, bf(peer_u), bf(peer_v), r2, e2, n1, e1, x2d, g_final,
                        tt=tl.ffn_tokens, ne=tl.ffn_experts, final_norm=(l == depth - 1))
    return x2d.reshape(b, s, d)
```
